```python
import jax, jax.numpy as jnp
from jax import lax
import numpy as np

D_MODEL = 2048
BATCH = 2
SEQ = 4096
DEPTH = 1

HEAD_DIM = 128
N_DELTA_HEADS = D_MODEL // (2 * HEAD_DIM)
N_ATTN_HEADS = D_MODEL // (2 * HEAD_DIM)
DELTA_WIDTH = N_DELTA_HEADS * HEAD_DIM
ATTN_WIDTH = N_ATTN_HEADS * HEAD_DIM
MIX_WIDTH = DELTA_WIDTH + ATTN_WIDTH
CONV_WIDTH = 4
CHUNK = 64
DILATED_PATTERNS = ((128, 1), (512, 4), (2048, 16))
ROPE_THETA = 10000.0
D_FF = ((8 * D_MODEL // 3 + 255) // 256) * 256
EPS = 1e-6

PROJ_SIZES = (DELTA_WIDTH, DELTA_WIDTH, DELTA_WIDTH, DELTA_WIDTH, N_DELTA_HEADS, N_DELTA_HEADS,
              ATTN_WIDTH, ATTN_WIDTH, ATTN_WIDTH)
PROJ_WIDTH = int(sum(PROJ_SIZES))
SPLIT_POINTS = tuple(int(s) for s in np.cumsum(PROJ_SIZES)[:-1])

kernel_name = "hybrid_deltanet_dilated_swa_layer"


def rms_norm(x, w):
    xf = x.astype(jnp.float32)
    y = xf * lax.rsqrt(jnp.mean(xf * xf, axis=-1, keepdims=True) + EPS)
    return (y * w.astype(jnp.float32)).astype(x.dtype)


def l2_norm(x):
    xf = x.astype(jnp.float32)
    return xf * lax.rsqrt(jnp.sum(xf * xf, axis=-1, keepdims=True) + EPS)


def rope(x, positions):
    half = x.shape[-1] // 2
    inv_freq = ROPE_THETA ** (-jnp.arange(half, dtype=jnp.float32) / half)
    ang = positions.astype(jnp.float32)[..., None] * inv_freq
    cos = jnp.cos(ang)[:, :, None, :]
    sin = jnp.sin(ang)[:, :, None, :]
    xf = x.astype(jnp.float32)
    x1, x2 = xf[..., :half], xf[..., half:]
    return jnp.concatenate([x1 * cos - x2 * sin, x2 * cos + x1 * sin], axis=-1).astype(x.dtype)


def causal_depthwise_conv_silu(x, w):
    k_len, chans = w.shape
    y = lax.conv_general_dilated(x, w[:, None, :].astype(x.dtype), window_strides=(1,),
                                 padding=[(k_len - 1, 0)],
                                 dimension_numbers=('NWC', 'WIO', 'NWC'),
                                 feature_group_count=chans)
    return jax.nn.silu(y)


def gated_delta_rule_chunked(q, k, v, g, beta):
    bsz, seq, heads, dk = q.shape
    dv = v.shape[-1]
    n = seq // CHUNK
    def to_chunks(t):
        return t.reshape(bsz, n, CHUNK, heads, t.shape[-1]).transpose(0, 3, 1, 2, 4)
    qc, kc, vc = to_chunks(q), to_chunks(k), to_chunks(v)
    gc = jnp.cumsum(g.reshape(bsz, n, CHUNK, heads).transpose(0, 3, 1, 2), axis=-1)
    bc = beta.reshape(bsz, n, CHUNK, heads).transpose(0, 3, 1, 2)
    idx = jnp.arange(CHUNK)
    causal = idx[:, None] >= idx[None, :]
    strict = idx[:, None] > idx[None, :]
    diff = gc[..., :, None] - gc[..., None, :]
    decay = jnp.where(causal, jnp.exp(jnp.where(causal, diff, 0.0)), 0.0)
    kk = jnp.einsum('bhncd,bhnmd->bhncm', kc, kc)
    lower = jnp.where(strict, bc[..., :, None] * kk * decay, 0.0)
    a_mat = lower + jnp.eye(CHUNK, dtype=jnp.float32)
    rhs = jnp.concatenate([vc * bc[..., None], kc * (bc * jnp.exp(gc))[..., None]], axis=-1)
    sol = lax.linalg.triangular_solve(a_mat, rhs, left_side=True, lower=True, unit_diagonal=True)
    u, w = sol[..., :dv], sol[..., dv:]
    qk = jnp.einsum('bhncd,bhnmd->bhncm', qc, kc) * decay
    q_dec = qc * jnp.exp(gc)[..., None]
    k_dec = kc * jnp.exp(gc[..., -1:] - gc)[..., None]
    chunk_decay = jnp.exp(gc[..., -1])

    def step(state, xs):
        u_c, w_c, qk_c, qd_c, kd_c, dec_c = xs
        v_new = u_c - jnp.einsum('bhcd,bhde->bhce', w_c, state)
        o_c = jnp.einsum('bhcd,bhde->bhce', qd_c, state) + jnp.einsum('bhcm,bhme->bhce', qk_c, v_new)
        state = state * dec_c[..., None, None] + jnp.einsum('bhcd,bhce->bhde', kd_c, v_new)
        return state, o_c

    xs = tuple(jnp.moveaxis(t, 2, 0) for t in (u, w, qk, q_dec, k_dec, chunk_decay))
    state0 = jnp.zeros((bsz, heads, dk, dv), jnp.float32)
    _, o = lax.scan(step, state0, xs)
    return o.transpose(1, 0, 3, 2, 4).reshape(bsz, seq, heads, dv)


def dilated_window_branch(q, k, v, window, dilation):
    bsz, seq, heads, hd = q.shape
    span = window // dilation
    blk = span
    unit = dilation * blk
    s_pad = -(-seq // unit) * unit
    nb = s_pad // unit
    def to_blocks(t):
        t = jnp.pad(t, [(0, 0), (0, s_pad - seq), (0, 0), (0, 0)])
        return t.reshape(bsz, nb, blk, dilation, heads, hd)
    def with_prev(t):
        prev = jnp.pad(t, [(0, 0), (1, 0), (0, 0), (0, 0), (0, 0), (0, 0)])[:, :-1]
        return jnp.concatenate([prev, t], axis=2)
    qb = to_blocks(q)
    kw = with_prev(to_blocks(k))
    vw = with_prev(to_blocks(v))
    s = jnp.einsum('bnqrhd,bnkrhd->bnrhqk', qb, kw,
                   preferred_element_type=jnp.float32) * (hd ** -0.5)
    qi = jnp.arange(blk)[:, None]
    ki = jnp.arange(2 * blk)[None, :]
    dist = qi + blk - ki
    band = (dist >= 0) & (dist <= span)
    mask = band[None] & ((jnp.arange(nb)[:, None, None] > 0) | (ki[None] >= blk))
    s = jnp.where(mask[None, :, None, None], s, -jnp.inf)
    m = jnp.max(s, axis=-1)
    p = jnp.exp(s - m[..., None])
    den = jnp.sum(p, axis=-1)
    o = jnp.einsum('bnrhqk,bnkrhd->bnqrhd', p, vw.astype(jnp.float32))
    m = m.transpose(0, 1, 4, 2, 3)
    den = den.transpose(0, 1, 4, 2, 3)
    o = o / den[..., None]
    o = o.reshape(bsz, s_pad, heads, hd)[:, :seq]
    m = m.reshape(bsz, s_pad, heads)[:, :seq]
    den = den.reshape(bsz, s_pad, heads)[:, :seq]
    return o, m, den


def dilated_mixture_attention(q, k, v):
    branches = [dilated_window_branch(q, k, v, w, d) for (w, d) in DILATED_PATTERNS]
    m_all = branches[0][1]
    for _, m_i, _ in branches[1:]:
        m_all = jnp.maximum(m_all, m_i)
    num = 0.0
    tot = 0.0
    for o_i, m_i, den_i in branches:
        wt = den_i * jnp.exp(m_i - m_all)
        num = num + wt[..., None] * o_i
        tot = tot + wt
    return num / tot[..., None]


def setup_inputs(seed: int = 0) -> dict:
    key = jax.random.key(seed)
    ks = jax.random.split(key, 20)
    f32 = jnp.float32
    x = jax.random.normal(ks[0], (BATCH, SEQ, D_MODEL), f32)
    offset = jax.random.randint(ks[1], (BATCH, 1), 0, 1024, dtype=jnp.int32)
    positions = (offset + jnp.arange(SEQ, dtype=jnp.int32)[None, :]).astype(jnp.int32)
    def gain(k, n):
        return 1.0 + 0.02 * jax.random.normal(k, (DEPTH, n), f32)
    attn_norm_w = gain(ks[2], D_MODEL)
    w_in = jax.random.normal(ks[3], (DEPTH, D_MODEL, PROJ_WIDTH), f32) * D_MODEL ** -0.5
    conv_w = jax.random.normal(ks[4], (DEPTH, CONV_WIDTH, 3 * DELTA_WIDTH), f32) * CONV_WIDTH ** -0.5
    a_log = jnp.log(jax.random.uniform(ks[5], (DEPTH, N_DELTA_HEADS), f32, 1.0, 16.0))
    dt = jnp.exp(jax.random.uniform(ks[6], (DEPTH, N_DELTA_HEADS), f32, np.log(1e-3), np.log(1e-1)))
    dt_bias = dt + jnp.log(-jnp.expm1(-dt))
    delta_out_norm_w = gain(ks[7], HEAD_DIM)
    q_norm_w = gain(ks[8], HEAD_DIM)
    k_norm_w = gain(ks[9], HEAD_DIM)
    attn_out_norm_w = gain(ks[10], HEAD_DIM)
    w_out = jax.random.normal(ks[11], (DEPTH, MIX_WIDTH, D_MODEL), f32) * MIX_WIDTH ** -0.5
    ffn_norm_w = gain(ks[12], D_MODEL)
    w_gate_up = jax.random.normal(ks[13], (DEPTH, D_MODEL, 2 * D_FF), f32) * D_MODEL ** -0.5
    w_down = jax.random.normal(ks[14], (DEPTH, D_FF, D_MODEL), f32) * D_FF ** -0.5
    return {'x': x, 'positions': positions, 'attn_norm_w': attn_norm_w, 'w_in': w_in,
            'conv_w': conv_w, 'a_log': a_log, 'dt_bias': dt_bias,
            'delta_out_norm_w': delta_out_norm_w, 'q_norm_w': q_norm_w, 'k_norm_w': k_norm_w,
            'attn_out_norm_w': attn_out_norm_w, 'w_out': w_out, 'ffn_norm_w': ffn_norm_w,
            'w_gate_up': w_gate_up, 'w_down': w_down}


def reference(x, positions, attn_norm_w, w_in, conv_w, a_log, dt_bias, delta_out_norm_w,
              q_norm_w, k_norm_w, attn_out_norm_w, w_out, ffn_norm_w, w_gate_up, w_down):
    bsz, seq, _ = x.shape
    for l in range(DEPTH):
        h = rms_norm(x, attn_norm_w[l])
        proj = h @ w_in[l].astype(h.dtype)
        qkv_raw, dz, db, da, aq, ak, av = jnp.split(
            proj, [3 * DELTA_WIDTH] + list(SPLIT_POINTS[3:]), axis=-1)
        qkv = causal_depthwise_conv_silu(qkv_raw, conv_w[l])
        dq, dk, dv = jnp.split(qkv, 3, axis=-1)
        dq = l2_norm(dq.reshape(bsz, seq, N_DELTA_HEADS, HEAD_DIM)) * (HEAD_DIM ** -0.5)
        dk = l2_norm(dk.reshape(bsz, seq, N_DELTA_HEADS, HEAD_DIM))
        dv = dv.reshape(bsz, seq, N_DELTA_HEADS, HEAD_DIM).astype(jnp.float32)
        beta = jax.nn.sigmoid(db.astype(jnp.float32))
        g = -jnp.exp(a_log[l].astype(jnp.float32)) * jax.nn.softplus(
            da.astype(jnp.float32) + dt_bias[l].astype(jnp.float32))
        o_a = gated_delta_rule_chunked(dq, dk, dv, g, beta)
        z = dz.reshape(bsz, seq, N_DELTA_HEADS, HEAD_DIM).astype(jnp.float32)
        o_a = (rms_norm(o_a, delta_out_norm_w[l]) * jax.nn.silu(z)).astype(x.dtype)
        aq = rope(rms_norm(aq.reshape(bsz, seq, N_ATTN_HEADS, HEAD_DIM), q_norm_w[l]), positions)
        ak = rope(rms_norm(ak.reshape(bsz, seq, N_ATTN_HEADS, HEAD_DIM), k_norm_w[l]), positions)
        av = av.reshape(bsz, seq, N_ATTN_HEADS, HEAD_DIM)
        o_b = dilated_mixture_attention(aq, ak, av)
        o_b = rms_norm(o_b, attn_out_norm_w[l]).astype(x.dtype)
        mixed = jnp.concatenate([o_a.reshape(bsz, seq, DELTA_WIDTH),
                                 o_b.reshape(bsz, seq, ATTN_WIDTH)], axis=-1)
        x = x + mixed @ w_out[l].astype(x.dtype)
        h = rms_norm(x, ffn_norm_w[l])
        gate, up = jnp.split(h @ w_gate_up[l].astype(h.dtype), 2, axis=-1)
        x = x + (jax.nn.silu(gate) * up) @ w_down[l].astype(x.dtype)
    return x
```

```python
import functools

import jax
import jax.numpy as jnp
import numpy as np
from jax import lax
from jax.experimental import pallas as pl
from jax.experimental.pallas import tpu as pltpu

HEAD_DIM = 128
CONV_WIDTH = 4
DILATED_PATTERNS = ((128, 1), (512, 4), (2048, 16))
ROPE_THETA = 10000.0
EPS = 1e-6

LANES = 128
SUBLANES = 8
DELTA_CHUNK = 128
NEG_BIG = -1e30

F32 = jnp.float32
BF16 = jnp.bfloat16

VMEM_LIMIT = 56 * 1024 * 1024


def _mm(a, b):
    return jnp.dot(a.astype(BF16), b.astype(BF16), preferred_element_type=F32)


def _mm_nt(a, b):
    return lax.dot_general(a.astype(BF16), b.astype(BF16), (((1,), (1,)), ((), ())),
                           preferred_element_type=F32)


def _mm_f32(a, b):
    return jnp.dot(a, b, preferred_element_type=F32, precision=lax.Precision.HIGHEST)


def _silu(x):
    return x * jax.nn.sigmoid(x)


def _rope_table_kernel(pos_ref, invf_ref, cos_ref, sin_ref):
    ang = pos_ref[...].astype(F32) * invf_ref[...]
    lane = lax.broadcasted_iota(jnp.int32, ang.shape, 1)
    cos_ref[...] = jnp.cos(ang)
    s = jnp.sin(ang)
    sin_ref[...] = jnp.where(lane < HEAD_DIM // 2, -s, s)


def _rope_tables(positions):
    tokens = positions.size
    tm = 1024
    half = HEAD_DIM // 2
    inv_freq = ROPE_THETA ** (-jnp.arange(half, dtype=F32) / half)
    invf = jnp.concatenate([inv_freq, inv_freq])[None, :]
    return pl.pallas_call(
        _rope_table_kernel,
        grid=(tokens // tm,),
        in_specs=[pl.BlockSpec((tm, 1), lambda i: (i, 0)),
                  pl.BlockSpec((1, HEAD_DIM), lambda i: (0, 0))],
        out_specs=[pl.BlockSpec((tm, HEAD_DIM), lambda i: (i, 0))] * 2,
        out_shape=[jax.ShapeDtypeStruct((tokens, HEAD_DIM), F32)] * 2,
        name="rope_tables",
    )(positions.reshape(tokens, 1), invf)


def _inproj_kernel(x_ref, nw_ref, w_ref, ws_ref, alog_ref, dtb_ref, qnw_ref, knw_ref,
                   cos_ref, sin_ref, out_ref, gates_ref, hn_ref, *, n_heads, aq_tile, ak_tile):
    j = pl.program_id(1)

    @pl.when(j == 0)
    def _():
        x = x_ref[...]
        y = x * lax.rsqrt(jnp.mean(x * x, axis=-1, keepdims=True) + EPS) * nw_ref[...]
        hn = y.astype(BF16)
        hn_ref[...] = hn
        gl = jnp.dot(hn, ws_ref[...], preferred_element_type=F32)
        lane = lax.broadcasted_iota(jnp.int32, gl.shape, 1)
        beta = jax.nn.sigmoid(gl)
        t = gl + dtb_ref[...]
        softplus = jnp.maximum(t, 0.0) + jnp.log1p(jnp.exp(-jnp.abs(t)))
        g = -jnp.exp(alog_ref[...]) * softplus
        gates_ref[...] = jnp.where(lane < n_heads, beta, g)

    acc = jnp.dot(hn_ref[...], w_ref[...], preferred_element_type=F32)
    is_qk = jnp.logical_or(j == aq_tile, j == ak_tile)

    @pl.when(is_qk)
    def _():
        nw = jnp.where(j == aq_tile, qnw_ref[...], knw_ref[...])
        cos = cos_ref[...]
        sin = sin_ref[...]
        for h in range(n_heads):
            xh = acc[:, h * HEAD_DIM:(h + 1) * HEAD_DIM]
            yh = xh * lax.rsqrt(jnp.mean(xh * xh, axis=-1, keepdims=True) + EPS) * nw
            out_ref[:, h * HEAD_DIM:(h + 1) * HEAD_DIM] = (
                yh * cos + pltpu.roll(yh, HEAD_DIM // 2, axis=1) * sin)

    @pl.when(jnp.logical_not(is_qk))
    def _():
        out_ref[...] = acc


def _inproj(x2, nw, w_main, w_small, alog, dtb, qnw, knw, cos_t, sin_t, n_heads):
    tokens, d_model = x2.shape
    width = n_heads * HEAD_DIM
    n_tiles = w_main.shape[1] // width
    tm = 512
    kern = functools.partial(_inproj_kernel, n_heads=n_heads, aq_tile=4, ak_tile=5)
    row = lambda i, j: (i, 0)
    const = lambda i, j: (0, 0)
    return pl.pallas_call(
        kern,
        grid=(tokens // tm, n_tiles),
        in_specs=[pl.BlockSpec((tm, d_model), row),
                  pl.BlockSpec((1, d_model), const),
                  pl.BlockSpec((d_model, width), lambda i, j: (0, j)),
                  pl.BlockSpec((d_model, LANES), const),
                  pl.BlockSpec((1, LANES), const),
                  pl.BlockSpec((1, LANES), const),
                  pl.BlockSpec((1, HEAD_DIM), const),
                  pl.BlockSpec((1, HEAD_DIM), const),
                  pl.BlockSpec((tm, HEAD_DIM), row),
                  pl.BlockSpec((tm, HEAD_DIM), row)],
        out_specs=[pl.BlockSpec((tm, width), lambda i, j: (i, j)),
                   pl.BlockSpec((tm, LANES), row)],
        out_shape=[jax.ShapeDtypeStruct((tokens, w_main.shape[1]), F32),
                   jax.ShapeDtypeStruct((tokens, LANES), F32)],
        scratch_shapes=[pltpu.VMEM((tm, d_model), BF16)],
        compiler_params=pltpu.CompilerParams(
            dimension_semantics=("parallel", "arbitrary"), vmem_limit_bytes=VMEM_LIMIT),
        name="inproj",
    )(x2, nw, w_main, w_small, alog, dtb, qnw, knw, cos_t, sin_t)


def _delta_kernel(q_ref, k_ref, v_ref, z_ref, gates_ref, cwq_ref, cwk_ref, cwv_ref, onw_ref,
                  o_ref, state_ref, carry_ref, xpad_ref, *, n_heads):
    h = pl.program_id(1)
    s_idx = pl.program_id(2)
    ts = q_ref.shape[0]
    c = DELTA_CHUNK
    n_chunks = ts // c

    @pl.when(s_idx == 0)
    def _():
        state_ref[...] = jnp.zeros_like(state_ref)
        carry_ref[...] = jnp.zeros_like(carry_ref)

    def conv_silu(x_ref, cw_ref, slot):
        xpad_ref[0:SUBLANES, :] = carry_ref[slot]
        xpad_ref[SUBLANES:, :] = x_ref[...]
        carry_ref[slot] = x_ref[ts - SUBLANES:ts, :]
        w = cw_ref[...]
        y = jnp.zeros((ts, HEAD_DIM), F32)
        for j in range(CONV_WIDTH):
            off = SUBLANES - (CONV_WIDTH - 1) + j
            y = y + xpad_ref[pl.ds(off, ts), :] * w[j:j + 1, :]
        return _silu(y)

    def l2n(t):
        return t * lax.rsqrt(jnp.sum(t * t, axis=-1, keepdims=True) + EPS)

    q_all = l2n(conv_silu(q_ref, cwq_ref, 0)) * (HEAD_DIM ** -0.5)
    k_all = l2n(conv_silu(k_ref, cwk_ref, 1))
    v_all = conv_silu(v_ref, cwv_ref, 2)

    gt = gates_ref[...]
    lane = lax.broadcasted_iota(jnp.int32, gt.shape, 1)
    beta_all = jnp.sum(jnp.where(lane == h, gt, 0.0), axis=-1, keepdims=True)
    g_all = jnp.sum(jnp.where(lane == h + n_heads, gt, 0.0), axis=-1, keepdims=True)

    ri = lax.broadcasted_iota(jnp.int32, (c, c), 0)
    ci = lax.broadcasted_iota(jnp.int32, (c, c), 1)
    incl = ri >= ci
    strict = ri > ci
    tri = incl.astype(F32)
    eye = (ri == ci).astype(F32)
    onw = onw_ref[...]

    for ch in range(n_chunks):
        rows = slice(ch * c, (ch + 1) * c)
        qc, kc, vc = q_all[rows], k_all[rows], v_all[rows]
        beta_c = beta_all[rows]
        gb = jnp.broadcast_to(g_all[rows], (c, c))
        gcb = _mm_f32(tri, gb)
        gcr = gcb.T
        decay = jnp.where(incl, jnp.exp(jnp.where(incl, gcb - gcr, 0.0)), 0.0)
        kk = _mm_nt(kc, kc)
        lower = jnp.where(strict, beta_c * kk * decay, 0.0)
        inv = eye - lower
        pw = _mm_f32(lower, lower)
        n_sq = int(np.log2(c)) - 1
        for it in range(n_sq):
            inv = inv + _mm_f32(inv, pw)
            if it + 1 < n_sq:
                pw = _mm_f32(pw, pw)
        egc = jnp.exp(gcb)
        u = _mm(inv, vc * beta_c)
        w = _mm(inv, kc * (beta_c * egc))
        qk = _mm_nt(qc, kc) * decay
        qd = qc * egc
        g_last = gcb[c - 1:c, :]
        kd_t = (kc * jnp.exp(g_last - gcb)).T
        state = state_ref[...]
        v_new = u - _mm(w, state)
        o_c = _mm(qd, state) + _mm(qk, v_new)
        state_ref[...] = state * jnp.exp(g_last) + _mm(kd_t, v_new)
        o_n = o_c * lax.rsqrt(jnp.mean(o_c * o_c, axis=-1, keepdims=True) + EPS) * onw
        o_ref[rows, :] = (o_n * _silu(z_ref[rows, :])).astype(o_ref.dtype)


def _delta(proj, gates, conv_w, onw, bsz, seq, n_heads):
    ts = 1024
    s_tiles = seq // ts
    kern = functools.partial(_delta_kernel, n_heads=n_heads)

    def col(seg):
        return lambda b, h, s: (b * s_tiles + s, seg * n_heads + h)

    def cw(seg):
        return lambda b, h, s: (0, seg * n_heads + h)

    tile = (ts, HEAD_DIM)
    return pl.pallas_call(
        kern,
        grid=(bsz, n_heads, s_tiles),
        in_specs=[pl.BlockSpec(tile, col(0)), pl.BlockSpec(tile, col(1)),
                  pl.BlockSpec(tile, col(2)), pl.BlockSpec(tile, col(3)),
                  pl.BlockSpec((ts, LANES), lambda b, h, s: (b * s_tiles + s, 0)),
                  pl.BlockSpec((CONV_WIDTH, HEAD_DIM), cw(0)),
                  pl.BlockSpec((CONV_WIDTH, HEAD_DIM), cw(1)),
                  pl.BlockSpec((CONV_WIDTH, HEAD_DIM), cw(2)),
                  pl.BlockSpec((1, HEAD_DIM), lambda b, h, s: (0, 0))],
        out_specs=pl.BlockSpec(tile, lambda b, h, s: (b * s_tiles + s, h)),
        out_shape=jax.ShapeDtypeStruct((bsz * seq, n_heads * HEAD_DIM), BF16),
        scratch_shapes=[pltpu.VMEM((HEAD_DIM, HEAD_DIM), F32),
                        pltpu.VMEM((3, SUBLANES, HEAD_DIM), F32),
                        pltpu.VMEM((ts + SUBLANES, HEAD_DIM), F32)],
        compiler_params=pltpu.CompilerParams(
            dimension_semantics=("parallel", "parallel", "arbitrary"),
            vmem_limit_bytes=VMEM_LIMIT),
        name="delta_rule",
    )(proj, proj, proj, proj, gates, conv_w, conv_w, conv_w, onw)


def _attn_kernel(q_ref, k_ref, v_ref, onw_ref, o_ref, acc_ref, m_ref, l_ref):
    seq = q_ref.shape[0]
    scale = HEAD_DIM ** -0.5
    m_ref[...] = jnp.full(m_ref.shape, NEG_BIG, F32)
    l_ref[...] = jnp.zeros(l_ref.shape, F32)
    acc_ref[...] = jnp.zeros(acc_ref.shape, F32)

    for window, dil in DILATED_PATTERNS:
        blk = window // dil
        nb = seq // (blk * dil)
        qi = lax.broadcasted_iota(jnp.int32, (blk, blk), 0)
        ki = lax.broadcasted_iota(jnp.int32, (blk, blk), 1)

        def body(idx, carry, dil=dil, blk=blk, nb=nb, qi=qi, ki=ki):
            r = idx // nb
            j = idx % nb
            start = j * (blk * dil) + r
            pstart = jnp.maximum(j - 1, 0) * (blk * dil) + r
            if dil == 1:
                cur, prev = pl.ds(start, blk), pl.ds(pstart, blk)
            else:
                cur, prev = pl.ds(start, blk, stride=dil), pl.ds(pstart, blk, stride=dil)
            qb = q_ref[cur, :]
            s_cur = _mm_nt(qb, k_ref[cur, :]) * scale
            s_prev = _mm_nt(qb, k_ref[prev, :]) * scale
            s_cur = jnp.where(ki <= qi, s_cur, NEG_BIG)
            s_prev = jnp.where(jnp.logical_and(ki >= qi, j > 0), s_prev, NEG_BIG)
            m_blk = jnp.maximum(jnp.max(s_cur, axis=-1, keepdims=True),
                                jnp.max(s_prev, axis=-1, keepdims=True))
            m_old = m_ref[cur, :]
            m_new = jnp.maximum(m_old, m_blk)
            alpha = jnp.exp(m_old - m_new)
            p_cur = jnp.exp(s_cur - m_new)
            p_prev = jnp.exp(s_prev - m_new)
            l_ref[cur, :] = alpha * l_ref[cur, :] + (jnp.sum(p_cur, axis=-1, keepdims=True)
                                                     + jnp.sum(p_prev, axis=-1, keepdims=True))
            acc_ref[cur, :] = (alpha * acc_ref[cur, :] + _mm(p_cur, v_ref[cur, :])
                               + _mm(p_prev, v_ref[prev, :]))
            m_ref[cur, :] = m_new
            return carry

        lax.fori_loop(0, dil * nb, body, 0)

    o = acc_ref[...] / l_ref[...]
    o = o * lax.rsqrt(jnp.mean(o * o, axis=-1, keepdims=True) + EPS) * onw_ref[...]
    o_ref[...] = o.astype(o_ref.dtype)


def _attention(proj, onw, bsz, seq, n_heads, col0):
    tile = (seq, HEAD_DIM)

    def col(seg):
        return lambda b, h: (b, col0 + seg * n_heads + h)

    return pl.pallas_call(
        _attn_kernel,
        grid=(bsz, n_heads),
        in_specs=[pl.BlockSpec(tile, col(0)), pl.BlockSpec(tile, col(1)),
                  pl.BlockSpec(tile, col(2)),
                  pl.BlockSpec((1, HEAD_DIM), lambda b, h: (0, 0))],
        out_specs=pl.BlockSpec(tile, lambda b, h: (b, h)),
        out_shape=jax.ShapeDtypeStruct((bsz * seq, n_heads * HEAD_DIM), BF16),
        scratch_shapes=[pltpu.VMEM(tile, F32)] * 3,
        compiler_params=pltpu.CompilerParams(
            dimension_semantics=("parallel", "parallel"), vmem_limit_bytes=VMEM_LIMIT),
        name="dilated_attention",
    )(proj, proj, proj, onw)


def _outproj_kernel(oa_ref, ob_ref, wa_ref, wb_ref, x_ref, fnw_ref, x1_ref, h2_ref):
    acc = (jnp.dot(oa_ref[...], wa_ref[...], preferred_element_type=F32)
           + jnp.dot(ob_ref[...], wb_ref[...], preferred_element_type=F32))
    x1 = x_ref[...] + acc
    x1_ref[...] = x1
    h2 = x1 * lax.rsqrt(jnp.mean(x1 * x1, axis=-1, keepdims=True) + EPS) * fnw_ref[...]
    h2_ref[...] = h2.astype(h2_ref.dtype)


def _outproj(o_a, o_b, w_out, x2, fnw):
    tokens, d_model = x2.shape
    wa_rows = o_a.shape[1]
    tm = 512
    row = lambda i: (i, 0)
    return pl.pallas_call(
        _outproj_kernel,
        grid=(tokens // tm,),
        in_specs=[pl.BlockSpec((tm, wa_rows), row),
                  pl.BlockSpec((tm, o_b.shape[1]), row),
                  pl.BlockSpec((wa_rows, d_model), lambda i: (0, 0)),
                  pl.BlockSpec((o_b.shape[1], d_model), lambda i: (1, 0)),
                  pl.BlockSpec((tm, d_model), row),
                  pl.BlockSpec((1, d_model), lambda i: (0, 0))],
        out_specs=[pl.BlockSpec((tm, d_model), row), pl.BlockSpec((tm, d_model), row)],
        out_shape=[jax.ShapeDtypeStruct((tokens, d_model), F32),
                   jax.ShapeDtypeStruct((tokens, d_model), BF16)],
        compiler_params=pltpu.CompilerParams(
            dimension_semantics=("parallel",), vmem_limit_bytes=VMEM_LIMIT),
        name="outproj",
    )(o_a, o_b, w_out, w_out, x2, fnw)


def _ffn_kernel(h_ref, wg_ref, wu_ref, wd_ref, x1_ref, o_ref):
    @pl.when(pl.program_id(1) == 0)
    def _():
        o_ref[...] = x1_ref[...]

    h = h_ref[...]
    g = jnp.dot(h, wg_ref[...], preferred_element_type=F32)
    u = jnp.dot(h, wu_ref[...], preferred_element_type=F32)
    a = (_silu(g) * u).astype(BF16)
    o_ref[...] += jnp.dot(a, wd_ref[...], preferred_element_type=F32)


def _ffn(h2, w_gate_up, w_down, x1):
    tokens, d_model = x1.shape
    d_ff = w_down.shape[0]
    tm, tf = 512, 512
    nf = d_ff // tf
    row = lambda i, f: (i, 0)
    return pl.pallas_call(
        _ffn_kernel,
        grid=(tokens // tm, nf),
        in_specs=[pl.BlockSpec((tm, d_model), row),
                  pl.BlockSpec((d_model, tf), lambda i, f: (0, f)),
                  pl.BlockSpec((d_model, tf), lambda i, f: (0, nf + f)),
                  pl.BlockSpec((tf, d_model), lambda i, f: (f, 0)),
                  pl.BlockSpec((tm, d_model), row)],
        out_specs=pl.BlockSpec((tm, d_model), row),
        out_shape=jax.ShapeDtypeStruct((tokens, d_model), F32),
        compiler_params=pltpu.CompilerParams(
            dimension_semantics=("parallel", "arbitrary"), vmem_limit_bytes=VMEM_LIMIT),
        name="swiglu",
    )(h2, w_gate_up, w_gate_up, w_down, x1)


def _pad_lanes(v, offset):
    out = jnp.zeros((1, LANES), F32)
    return lax.dynamic_update_slice(out, v.astype(F32)[None, :], (0, offset))


def kernel(x, positions, attn_norm_w, w_in, conv_w, a_log, dt_bias, delta_out_norm_w, q_norm_w,
           k_norm_w, attn_out_norm_w, w_out, ffn_norm_w, w_gate_up, w_down):
    bsz, seq, d_model = x.shape
    depth = w_in.shape[0]
    n_heads = a_log.shape[1]
    width = n_heads * HEAD_DIM
    gate_lo, gate_hi = 4 * width, 4 * width + 2 * n_heads
    assert 2 * n_heads <= LANES and w_in.shape[2] == gate_hi + 3 * width

    cos_t, sin_t = _rope_tables(positions)
    x2 = x.reshape(bsz * seq, d_model)
    for l in range(depth):
        w_l = w_in[l]
        w_main = jnp.concatenate([w_l[:, :gate_lo], w_l[:, gate_hi:]], axis=1).astype(BF16)
        w_small = jnp.pad(w_l[:, gate_lo:gate_hi], ((0, 0), (0, LANES - 2 * n_heads))).astype(BF16)
        proj, gates = _inproj(
            x2, attn_norm_w[l][None, :], w_main, w_small,
            _pad_lanes(a_log[l], n_heads), _pad_lanes(dt_bias[l], n_heads),
            q_norm_w[l][None, :], k_norm_w[l][None, :], cos_t, sin_t, n_heads)
        o_a = _delta(proj, gates, conv_w[l], delta_out_norm_w[l][None, :], bsz, seq, n_heads)
        o_b = _attention(proj, attn_out_norm_w[l][None, :], bsz, seq, n_heads, 4 * n_heads)
        x1, h2 = _outproj(o_a, o_b, w_out[l].astype(BF16), x2, ffn_norm_w[l][None, :])
        x2 = _ffn(h2, w_gate_up[l].astype(BF16), w_down[l].astype(BF16), x1)
    return x2.reshape(bsz, seq, d_model)
```

```python
import functools

import jax
import jax.numpy as jnp
from jax import lax
from jax.experimental import pallas as pl
from jax.experimental.pallas import tpu as pltpu

HEAD_DIM = 128
CONV_WIDTH = 4
DILATED_PATTERNS = ((128, 1), (512, 4), (2048, 16))
ROPE_THETA = 10000.0
EPS = 1e-6

LANES = 128
SUBLANES = 8
DELTA_CHUNK = 128
NEG_BIG = -1e30
ATTN_UNROLL = 8

F32 = jnp.float32
BF16 = jnp.bfloat16

VMEM_LIMIT = 56 * 1024 * 1024


def _mm(a, b):
    return jnp.dot(a.astype(BF16), b.astype(BF16), preferred_element_type=F32)


def _mm_nt(a, b):
    return lax.dot_general(a.astype(BF16), b.astype(BF16), (((1,), (1,)), ((), ())),
                           preferred_element_type=F32)


def _mm_f32(a, b):
    return jnp.dot(a, b, preferred_element_type=F32, precision=lax.Precision.HIGHEST)


def _silu(x):
    return x * jax.nn.sigmoid(x)


def _rope_table_kernel(pos_ref, invf_ref, cos_ref, sin_ref):
    ang = pos_ref[...].astype(F32) * invf_ref[...]
    lane = lax.broadcasted_iota(jnp.int32, ang.shape, 1)
    cos_ref[...] = jnp.cos(ang)
    s = jnp.sin(ang)
    sin_ref[...] = jnp.where(lane < HEAD_DIM // 2, -s, s)


def _rope_tables(positions):
    tokens = positions.size
    tm = 1024
    half = HEAD_DIM // 2
    inv_freq = ROPE_THETA ** (-jnp.arange(half, dtype=F32) / half)
    invf = jnp.concatenate([inv_freq, inv_freq])[None, :]
    return pl.pallas_call(
        _rope_table_kernel,
        grid=(tokens // tm,),
        in_specs=[pl.BlockSpec((tm, 1), lambda i: (i, 0)),
                  pl.BlockSpec((1, HEAD_DIM), lambda i: (0, 0))],
        out_specs=[pl.BlockSpec((tm, HEAD_DIM), lambda i: (i, 0))] * 2,
        out_shape=[jax.ShapeDtypeStruct((tokens, HEAD_DIM), F32)] * 2,
        name="rope_tables",
    )(positions.reshape(tokens, 1), invf)


def _inproj_kernel(x_ref, nw_ref, w_ref, ws_ref, alog_ref, dtb_ref, qnw_ref, knw_ref,
                   cos_ref, sin_ref, out_ref, gates_ref, hn_ref, *, n_heads, aq_tile, ak_tile):
    j = pl.program_id(1)

    @pl.when(j == 0)
    def _():
        x = x_ref[...]
        y = x * lax.rsqrt(jnp.mean(x * x, axis=-1, keepdims=True) + EPS) * nw_ref[...]
        hn = y.astype(BF16)
        hn_ref[...] = hn
        gl = jnp.dot(hn, ws_ref[...], preferred_element_type=F32)
        lane = lax.broadcasted_iota(jnp.int32, gl.shape, 1)
        beta = jax.nn.sigmoid(gl)
        t = gl + dtb_ref[...]
        softplus = jnp.maximum(t, 0.0) + jnp.log1p(jnp.exp(-jnp.abs(t)))
        g = -jnp.exp(alog_ref[...]) * softplus
        tm = g.shape[0]
        ri = lax.broadcasted_iota(jnp.int32, (tm, tm), 0)
        ci = lax.broadcasted_iota(jnp.int32, (tm, tm), 1)
        same_chunk = (ri // DELTA_CHUNK) == (ci // DELTA_CHUNK)
        tri = jnp.logical_and(ri >= ci, same_chunk).astype(F32)
        gc = _mm_f32(tri, g)
        gates_ref[...] = jnp.where(lane < n_heads, beta, gc)

    acc = jnp.dot(hn_ref[...], w_ref[...], preferred_element_type=F32)
    is_qk = jnp.logical_or(j == aq_tile, j == ak_tile)

    @pl.when(is_qk)
    def _():
        nw = jnp.where(j == aq_tile, qnw_ref[...], knw_ref[...])
        cos = cos_ref[...]
        sin = sin_ref[...]
        for h in range(n_heads):
            xh = acc[:, h * HEAD_DIM:(h + 1) * HEAD_DIM]
            yh = xh * lax.rsqrt(jnp.mean(xh * xh, axis=-1, keepdims=True) + EPS) * nw
            out_ref[:, h * HEAD_DIM:(h + 1) * HEAD_DIM] = (
                yh * cos + pltpu.roll(yh, HEAD_DIM // 2, axis=1) * sin)

    @pl.when(jnp.logical_not(is_qk))
    def _():
        out_ref[...] = acc


def _inproj(x2, nw, w_main, w_small, alog, dtb, qnw, knw, cos_t, sin_t, n_heads):
    tokens, d_model = x2.shape
    width = n_heads * HEAD_DIM
    n_tiles = w_main.shape[1] // width
    tm = 512
    kern = functools.partial(_inproj_kernel, n_heads=n_heads, aq_tile=4, ak_tile=5)
    row = lambda i, j: (i, 0)
    const = lambda i, j: (0, 0)
    return pl.pallas_call(
        kern,
        grid=(tokens // tm, n_tiles),
        in_specs=[pl.BlockSpec((tm, d_model), row),
                  pl.BlockSpec((1, d_model), const),
                  pl.BlockSpec((d_model, width), lambda i, j: (0, j)),
                  pl.BlockSpec((d_model, LANES), const),
                  pl.BlockSpec((1, LANES), const),
                  pl.BlockSpec((1, LANES), const),
                  pl.BlockSpec((1, HEAD_DIM), const),
                  pl.BlockSpec((1, HEAD_DIM), const),
                  pl.BlockSpec((tm, HEAD_DIM), row),
                  pl.BlockSpec((tm, HEAD_DIM), row)],
        out_specs=[pl.BlockSpec((tm, width), lambda i, j: (i, j)),
                   pl.BlockSpec((tm, LANES), row)],
        out_shape=[jax.ShapeDtypeStruct((tokens, w_main.shape[1]), F32),
                   jax.ShapeDtypeStruct((tokens, LANES), F32)],
        scratch_shapes=[pltpu.VMEM((tm, d_model), BF16)],
        compiler_params=pltpu.CompilerParams(
            dimension_semantics=("parallel", "arbitrary"), vmem_limit_bytes=VMEM_LIMIT),
        name="inproj",
    )(x2, nw, w_main, w_small, alog, dtb, qnw, knw, cos_t, sin_t)


def _delta_kernel(q_ref, k_ref, v_ref, z_ref, gates_ref, cwq_ref, cwk_ref, cwv_ref, onw_ref,
                  o_ref, state_ref, carry_ref, xpad_ref, *, n_heads):
    h = pl.program_id(1)
    s_idx = pl.program_id(2)
    ts = q_ref.shape[0]
    c = DELTA_CHUNK
    n_chunks = ts // c

    @pl.when(s_idx == 0)
    def _():
        state_ref[...] = jnp.zeros_like(state_ref)
        carry_ref[...] = jnp.zeros_like(carry_ref)

    def conv_silu(x_ref, cw_ref, slot):
        xpad_ref[0:SUBLANES, :] = carry_ref[slot]
        xpad_ref[SUBLANES:, :] = x_ref[...]
        carry_ref[slot] = x_ref[ts - SUBLANES:ts, :]
        w = cw_ref[...]
        y = jnp.zeros((ts, HEAD_DIM), F32)
        for j in range(CONV_WIDTH):
            off = SUBLANES - (CONV_WIDTH - 1) + j
            y = y + xpad_ref[pl.ds(off, ts), :] * w[j:j + 1, :]
        return _silu(y)

    def l2n(t):
        return t * lax.rsqrt(jnp.sum(t * t, axis=-1, keepdims=True) + EPS)

    q_all = l2n(conv_silu(q_ref, cwq_ref, 0)) * (HEAD_DIM ** -0.5)
    k_all = l2n(conv_silu(k_ref, cwk_ref, 1))
    v_all = conv_silu(v_ref, cwv_ref, 2)

    gt = gates_ref[...]
    lane = lax.broadcasted_iota(jnp.int32, gt.shape, 1)
    beta_all = jnp.sum(jnp.where(lane == h, gt, 0.0), axis=-1, keepdims=True)
    gc_all = jnp.sum(jnp.where(lane == h + n_heads, gt, 0.0), axis=-1, keepdims=True)

    ri = lax.broadcasted_iota(jnp.int32, (c, c), 0)
    ci = lax.broadcasted_iota(jnp.int32, (c, c), 1)
    incl = ri >= ci
    strict = ri > ci
    eye = (ri == ci).astype(F32)
    onw = onw_ref[...]
    state = state_ref[...]

    chunks = range(n_chunks)
    rows = [slice(ch * c, (ch + 1) * c) for ch in chunks]
    qs = [q_all[r] for r in rows]
    ks = [k_all[r] for r in rows]
    vs = [v_all[r] for r in rows]
    betas = [beta_all[r] for r in rows]
    gcbs = [jnp.broadcast_to(gc_all[r], (c, c)) for r in rows]
    kks = [_mm_nt(ks[ch], ks[ch]) for ch in chunks]
    qks = [_mm_nt(qs[ch], ks[ch]) for ch in chunks]
    decays = []
    for ch in chunks:
        gcr = gcbs[ch].T
        decays.append(jnp.where(incl, jnp.exp(jnp.where(incl, gcbs[ch] - gcr, 0.0)), 0.0))
    lowers = [jnp.where(strict, betas[ch] * kks[ch] * decays[ch], 0.0) for ch in chunks]
    same2 = (ri // 2) == (ci // 2)
    invs = [eye - jnp.where(same2, lowers[ch], 0.0) for ch in chunks]
    b = 2
    while b < c:
        off = jnp.logical_and((ri // (2 * b)) == (ci // (2 * b)), (ri // b) != (ci // b))
        tmp = [_mm(jnp.where(off, lowers[ch], 0.0), invs[ch]) for ch in chunks]
        invs = [invs[ch] - _mm(invs[ch], tmp[ch]) for ch in chunks]
        b *= 2
    egcs = [jnp.exp(gcbs[ch]) for ch in chunks]
    us = [_mm(invs[ch], vs[ch] * betas[ch]) for ch in chunks]
    ws = [_mm(invs[ch], ks[ch] * (betas[ch] * egcs[ch])) for ch in chunks]
    qks = [qks[ch] * decays[ch] for ch in chunks]
    qds = [qs[ch] * egcs[ch] for ch in chunks]
    g_lasts = [gcbs[ch][c - 1:c, :] for ch in chunks]
    kd_ts = [(ks[ch] * jnp.exp(g_lasts[ch] - gcbs[ch])).T for ch in chunks]

    outs = []
    for ch in chunks:
        v_new = us[ch] - _mm(ws[ch], state)
        outs.append(_mm(qds[ch], state) + _mm(qks[ch], v_new))
        state = state * jnp.exp(g_lasts[ch]) + _mm(kd_ts[ch], v_new)
    state_ref[...] = state

    for ch in chunks:
        o_c = outs[ch]
        o_n = o_c * lax.rsqrt(jnp.mean(o_c * o_c, axis=-1, keepdims=True) + EPS) * onw
        o_ref[rows[ch], :] = (o_n * _silu(z_ref[rows[ch], :])).astype(o_ref.dtype)


def _delta(proj, gates, conv_w, onw, bsz, seq, n_heads):
    ts = 1024
    s_tiles = seq // ts
    kern = functools.partial(_delta_kernel, n_heads=n_heads)

    def col(seg):
        return lambda b, h, s: (b * s_tiles + s, seg * n_heads + h)

    def cw(seg):
        return lambda b, h, s: (0, seg * n_heads + h)

    tile = (ts, HEAD_DIM)
    return pl.pallas_call(
        kern,
        grid=(bsz, n_heads, s_tiles),
        in_specs=[pl.BlockSpec(tile, col(0)), pl.BlockSpec(tile, col(1)),
                  pl.BlockSpec(tile, col(2)), pl.BlockSpec(tile, col(3)),
                  pl.BlockSpec((ts, LANES), lambda b, h, s: (b * s_tiles + s, 0)),
                  pl.BlockSpec((CONV_WIDTH, HEAD_DIM), cw(0)),
                  pl.BlockSpec((CONV_WIDTH, HEAD_DIM), cw(1)),
                  pl.BlockSpec((CONV_WIDTH, HEAD_DIM), cw(2)),
                  pl.BlockSpec((1, HEAD_DIM), lambda b, h, s: (0, 0))],
        out_specs=pl.BlockSpec(tile, lambda b, h, s: (b * s_tiles + s, h)),
        out_shape=jax.ShapeDtypeStruct((bsz * seq, n_heads * HEAD_DIM), BF16),
        scratch_shapes=[pltpu.VMEM((HEAD_DIM, HEAD_DIM), F32),
                        pltpu.VMEM((3, SUBLANES, HEAD_DIM), F32),
                        pltpu.VMEM((ts + SUBLANES, HEAD_DIM), F32)],
        compiler_params=pltpu.CompilerParams(
            dimension_semantics=("parallel", "parallel", "arbitrary"),
            vmem_limit_bytes=VMEM_LIMIT),
        name="delta_rule",
    )(proj, proj, proj, proj, gates, conv_w, conv_w, conv_w, onw)


def _attn_kernel(q_ref, k_ref, v_ref, onw_ref, o_ref, acc_ref, m_ref, l_ref):
    seq = q_ref.shape[0]
    scale = HEAD_DIM ** -0.5
    onw = onw_ref[...]
    order = sorted(DILATED_PATTERNS, key=lambda p: -p[1])

    for pi, (window, dil) in enumerate(order):
        first, last = pi == 0, pi == len(order) - 1
        blk = window // dil
        nb = seq // (blk * dil)
        qi = lax.broadcasted_iota(jnp.int32, (blk, blk), 0)
        ki = lax.broadcasted_iota(jnp.int32, (blk, blk), 1)

        def body(it, carry, first=first, last=last, dil=dil, blk=blk, nb=nb, qi=qi, ki=ki):
            loaded = []
            for u in range(ATTN_UNROLL):
                idx = it * ATTN_UNROLL + u
                r = idx // nb
                j = idx % nb
                start = j * (blk * dil) + r
                pstart = jnp.maximum(j - 1, 0) * (blk * dil) + r
                if dil == 1:
                    cur = pl.ds(pl.multiple_of(start, blk), blk)
                    prev = pl.ds(pl.multiple_of(pstart, blk), blk)
                else:
                    cur, prev = pl.ds(start, blk, stride=dil), pl.ds(pstart, blk, stride=dil)
                tiles = (q_ref[cur, :] * scale, k_ref[cur, :], k_ref[prev, :],
                         v_ref[cur, :], v_ref[prev, :])
                old = None if first else (m_ref[cur, :], l_ref[cur, :], acc_ref[cur, :])
                loaded.append((cur, j, tiles, old))

            scores = []
            for cur, j, (qb, kc, kp, vc, vp), old in loaded:
                s_cur = jnp.where(ki <= qi, _mm_nt(qb, kc), NEG_BIG)
                s_prev = jnp.where(jnp.logical_and(ki >= qi, j > 0), _mm_nt(qb, kp), NEG_BIG)
                scores.append((s_cur, s_prev))

            probs = []
            for (cur, j, tiles, old), (s_cur, s_prev) in zip(loaded, scores):
                m_new = jnp.max(jnp.maximum(s_cur, s_prev), axis=-1, keepdims=True)
                if old is not None:
                    m_new = jnp.maximum(old[0], m_new)
                p_cur = jnp.exp(s_cur - m_new)
                p_prev = jnp.exp(s_prev - m_new)
                l_new = jnp.sum(p_cur + p_prev, axis=-1, keepdims=True)
                probs.append((m_new, p_cur, p_prev, l_new))

            results = []
            for (cur, j, (qb, kc, kp, vc, vp), old), (m_new, p_cur, p_prev, l_new) in zip(loaded, probs):
                acc_new = _mm(p_cur, vc) + _mm(p_prev, vp)
                if old is not None:
                    alpha = jnp.exp(old[0] - m_new)
                    l_new = alpha * old[1] + l_new
                    acc_new = alpha * old[2] + acc_new
                results.append((cur, m_new, l_new, acc_new))

            for cur, m_new, l_new, acc_new in results:
                if last:
                    o = acc_new / l_new
                    o = o * lax.rsqrt(jnp.mean(o * o, axis=-1, keepdims=True) + EPS) * onw
                    o_ref[cur, :] = o.astype(o_ref.dtype)
                else:
                    m_ref[cur, :] = jnp.broadcast_to(m_new, (blk, HEAD_DIM))
                    l_ref[cur, :] = jnp.broadcast_to(l_new, (blk, HEAD_DIM))
                    acc_ref[cur, :] = acc_new
            return carry

        lax.fori_loop(0, dil * nb // ATTN_UNROLL, body, 0)


def _attention(proj, onw, bsz, seq, n_heads, col0):
    tile = (seq, HEAD_DIM)

    def col(seg):
        return lambda b, h: (b, col0 + seg * n_heads + h)

    return pl.pallas_call(
        _attn_kernel,
        grid=(bsz, n_heads),
        in_specs=[pl.BlockSpec(tile, col(0)), pl.BlockSpec(tile, col(1)),
                  pl.BlockSpec(tile, col(2)),
                  pl.BlockSpec((1, HEAD_DIM), lambda b, h: (0, 0))],
        out_specs=pl.BlockSpec(tile, lambda b, h: (b, h)),
        out_shape=jax.ShapeDtypeStruct((bsz * seq, n_heads * HEAD_DIM), BF16),
        scratch_shapes=[pltpu.VMEM(tile, F32)] * 3,
        compiler_params=pltpu.CompilerParams(
            dimension_semantics=("parallel", "parallel"), vmem_limit_bytes=VMEM_LIMIT),
        name="dilated_attention",
    )(proj, proj, proj, onw)


def _outproj_kernel(oa_ref, ob_ref, wa_ref, wb_ref, x_ref, fnw_ref, x1_ref, h2_ref):
    acc = (jnp.dot(oa_ref[...], wa_ref[...], preferred_element_type=F32)
           + jnp.dot(ob_ref[...], wb_ref[...], preferred_element_type=F32))
    x1 = x_ref[...] + acc
    x1_ref[...] = x1
    h2 = x1 * lax.rsqrt(jnp.mean(x1 * x1, axis=-1, keepdims=True) + EPS) * fnw_ref[...]
    h2_ref[...] = h2.astype(h2_ref.dtype)


def _outproj(o_a, o_b, w_out, x2, fnw):
    tokens, d_model = x2.shape
    wa_rows = o_a.shape[1]
    tm = 512
    row = lambda i: (i, 0)
    return pl.pallas_call(
        _outproj_kernel,
        grid=(tokens // tm,),
        in_specs=[pl.BlockSpec((tm, wa_rows), row),
                  pl.BlockSpec((tm, o_b.shape[1]), row),
                  pl.BlockSpec((wa_rows, d_model), lambda i: (0, 0)),
                  pl.BlockSpec((o_b.shape[1], d_model), lambda i: (1, 0)),
                  pl.BlockSpec((tm, d_model), row),
                  pl.BlockSpec((1, d_model), lambda i: (0, 0))],
        out_specs=[pl.BlockSpec((tm, d_model), row), pl.BlockSpec((tm, d_model), row)],
        out_shape=[jax.ShapeDtypeStruct((tokens, d_model), F32),
                   jax.ShapeDtypeStruct((tokens, d_model), BF16)],
        compiler_params=pltpu.CompilerParams(
            dimension_semantics=("parallel",), vmem_limit_bytes=VMEM_LIMIT),
        name="outproj",
    )(o_a, o_b, w_out, w_out, x2, fnw)


def _ffn_kernel(h_ref, wg_ref, wu_ref, wd_ref, x1_ref, o_ref):
    @pl.when(pl.program_id(1) == 0)
    def _():
        o_ref[...] = x1_ref[...]

    h = h_ref[...]
    g = jnp.dot(h, wg_ref[...], preferred_element_type=F32)
    u = jnp.dot(h, wu_ref[...], preferred_element_type=F32)
    a = (_silu(g) * u).astype(BF16)
    o_ref[...] += jnp.dot(a, wd_ref[...], preferred_element_type=F32)


def _ffn(h2, w_gate_up, w_down, x1):
    tokens, d_model = x1.shape
    d_ff = w_down.shape[0]
    tm, tf = 512, 512
    nf = d_ff // tf
    row = lambda i, f: (i, 0)
    return pl.pallas_call(
        _ffn_kernel,
        grid=(tokens // tm, nf),
        in_specs=[pl.BlockSpec((tm, d_model), row),
                  pl.BlockSpec((d_model, tf), lambda i, f: (0, f)),
                  pl.BlockSpec((d_model, tf), lambda i, f: (0, nf + f)),
                  pl.BlockSpec((tf, d_model), lambda i, f: (f, 0)),
                  pl.BlockSpec((tm, d_model), row)],
        out_specs=pl.BlockSpec((tm, d_model), row),
        out_shape=jax.ShapeDtypeStruct((tokens, d_model), F32),
        compiler_params=pltpu.CompilerParams(
            dimension_semantics=("parallel", "arbitrary"), vmem_limit_bytes=VMEM_LIMIT),
        name="swiglu",
    )(h2, w_gate_up, w_gate_up, w_down, x1)


def _pad_lanes(v, offset):
    out = jnp.zeros((1, LANES), F32)
    return lax.dynamic_update_slice(out, v.astype(F32)[None, :], (0, offset))


def kernel(x, positions, attn_norm_w, w_in, conv_w, a_log, dt_bias, delta_out_norm_w, q_norm_w,
           k_norm_w, attn_out_norm_w, w_out, ffn_norm_w, w_gate_up, w_down):
    bsz, seq, d_model = x.shape
    depth = w_in.shape[0]
    n_heads = a_log.shape[1]
    width = n_heads * HEAD_DIM
    gate_lo, gate_hi = 4 * width, 4 * width + 2 * n_heads
    assert 2 * n_heads <= LANES and w_in.shape[2] == gate_hi + 3 * width

    cos_t, sin_t = _rope_tables(positions)
    x2 = x.reshape(bsz * seq, d_model)
    for l in range(depth):
        w_l = w_in[l]
        w_main = jnp.concatenate([w_l[:, :gate_lo], w_l[:, gate_hi:]], axis=1).astype(BF16)
        w_small = jnp.pad(w_l[:, gate_lo:gate_hi], ((0, 0), (0, LANES - 2 * n_heads))).astype(BF16)
        proj, gates = _inproj(
            x2, attn_norm_w[l][None, :], w_main, w_small,
            _pad_lanes(a_log[l], n_heads), _pad_lanes(dt_bias[l], n_heads),
            q_norm_w[l][None, :], k_norm_w[l][None, :], cos_t, sin_t, n_heads)
        o_a = _delta(proj, gates, conv_w[l], delta_out_norm_w[l][None, :], bsz, seq, n_heads)
        o_b = _attention(proj, attn_out_norm_w[l][None, :], bsz, seq, n_heads, 4 * n_heads)
        x1, h2 = _outproj(o_a, o_b, w_out[l].astype(BF16), x2, ffn_norm_w[l][None, :])
        x2 = _ffn(h2, w_gate_up[l].astype(BF16), w_down[l].astype(BF16), x1)
    return x2.reshape(bsz, seq, d_model)
```

```python
import functools

import jax
import jax.numpy as jnp
from jax import lax
from jax.experimental import pallas as pl
from jax.experimental.pallas import tpu as pltpu

HEAD_DIM = 128
CONV_WIDTH = 4
DILATED_PATTERNS = ((128, 1), (512, 4), (2048, 16))
ROPE_THETA = 10000.0
EPS = 1e-6

LANES = 128
SUBLANES = 8
DELTA_CHUNK = 128
NEG_BIG = -1e30
ATTN_UNROLL = 8

F32 = jnp.float32
BF16 = jnp.bfloat16

VMEM_LIMIT = 60 * 1024 * 1024


def _mm(a, b):
    return jnp.dot(a.astype(BF16), b.astype(BF16), preferred_element_type=F32)


def _mm_nt(a, b):
    return lax.dot_general(a.astype(BF16), b.astype(BF16), (((1,), (1,)), ((), ())),
                           preferred_element_type=F32)


def _split2(x):
    hi = x.astype(BF16)
    lo = (x - hi.astype(F32)).astype(BF16)
    return jnp.concatenate([hi, lo], axis=1)


def _silu(x):
    return x * jax.nn.sigmoid(x)


def _rope_table_kernel(pos_ref, invf_ref, cos_ref, sin_ref):
    ang = pos_ref[...].astype(F32) * invf_ref[...]
    lane = lax.broadcasted_iota(jnp.int32, ang.shape, 1)
    cos_ref[...] = jnp.cos(ang)
    s = jnp.sin(ang)
    sin_ref[...] = jnp.where(lane < HEAD_DIM // 2, -s, s)


def _rope_tables(positions):
    tokens = positions.size
    tm = 1024
    half = HEAD_DIM // 2
    inv_freq = ROPE_THETA ** (-jnp.arange(half, dtype=F32) / half)
    invf = jnp.concatenate([inv_freq, inv_freq])[None, :]
    return pl.pallas_call(
        _rope_table_kernel,
        grid=(tokens // tm,),
        in_specs=[pl.BlockSpec((tm, 1), lambda i: (i, 0)),
                  pl.BlockSpec((1, HEAD_DIM), lambda i: (0, 0))],
        out_specs=[pl.BlockSpec((tm, HEAD_DIM), lambda i: (i, 0))] * 2,
        out_shape=[jax.ShapeDtypeStruct((tokens, HEAD_DIM), F32)] * 2,
        name="rope_tables",
    )(positions.reshape(tokens, 1), invf)


def _inproj_kernel(x_ref, nw_ref, wd_ref, wa_ref, wg_ref, tri_ref, alog_ref, dtb_ref, qnw_ref,
                   knw_ref, cos_ref, sin_ref, out_ref, gates_ref, hn_ref, *, n_heads, n_delta_tiles):
    j = pl.program_id(1)

    @pl.when(j == 0)
    def _():
        x = x_ref[...]
        y = x * lax.rsqrt(jnp.mean(x * x, axis=-1, keepdims=True) + EPS) * nw_ref[...]
        hn = y.astype(BF16)
        hn_ref[...] = hn
        gl = jnp.dot(hn, wg_ref[...], preferred_element_type=F32)
        lane = lax.broadcasted_iota(jnp.int32, gl.shape, 1)
        beta = jax.nn.sigmoid(gl)
        t = gl + dtb_ref[...]
        softplus = jnp.maximum(t, 0.0) + jnp.log1p(jnp.exp(-jnp.abs(t)))
        g = -jnp.exp(alog_ref[...]) * softplus
        g_hi = g.astype(BF16)
        rest = g - g_hi.astype(F32)
        g_mid = rest.astype(BF16)
        g_lo = (rest - g_mid.astype(F32)).astype(BF16)
        parts = jnp.dot(tri_ref[...], jnp.concatenate([g_hi, g_mid, g_lo], axis=1),
                        preferred_element_type=F32)
        gc = parts[:, :LANES] + parts[:, LANES:2 * LANES] + parts[:, 2 * LANES:]
        gates_ref[...] = jnp.where(lane < n_heads, beta, gc)

    @pl.when(j < n_delta_tiles)
    def _():
        out_ref[...] = jnp.dot(hn_ref[...], wd_ref[...], preferred_element_type=F32)

    @pl.when(j == n_delta_tiles + 2)
    def _():
        out_ref[...] = jnp.dot(hn_ref[...], wa_ref[...], preferred_element_type=F32)

    @pl.when(jnp.logical_or(j == n_delta_tiles, j == n_delta_tiles + 1))
    def _():
        acc = jnp.dot(hn_ref[...], wa_ref[...], preferred_element_type=F32)
        nw = jnp.where(j == n_delta_tiles, qnw_ref[...], knw_ref[...])
        cos = cos_ref[...]
        sin = sin_ref[...]
        pi = lax.broadcasted_iota(jnp.int32, (2 * HEAD_DIM, HEAD_DIM), 0)
        pj = lax.broadcasted_iota(jnp.int32, (2 * HEAD_DIM, HEAD_DIM), 1)
        ones2 = jnp.ones((2 * HEAD_DIM, HEAD_DIM), BF16)
        rot2 = ((pi % HEAD_DIM) == ((pj + HEAD_DIM // 2) % HEAD_DIM)).astype(BF16)
        for h in range(n_heads):
            xh = acc[:, h * HEAD_DIM:(h + 1) * HEAD_DIM]
            mean_sq = jnp.dot(_split2(xh * xh), ones2, preferred_element_type=F32) * (1.0 / HEAD_DIM)
            yh = xh * lax.rsqrt(mean_sq + EPS) * nw
            yh_rot = jnp.dot(_split2(yh), rot2, preferred_element_type=F32)
            out_ref[:, h * HEAD_DIM:(h + 1) * HEAD_DIM] = yh * cos + yh_rot * sin


def _inproj(x2, nw, w_all, w_attn, alog, dtb, qnw, knw, cos_t, sin_t, n_heads):
    tokens, d_model = x2.shape
    width = n_heads * HEAD_DIM
    n_delta_tiles = 4
    n_tiles = n_delta_tiles + w_attn.shape[1] // width
    tm = 1024
    ri = lax.broadcasted_iota(jnp.int32, (tm, tm), 0)
    ci = lax.broadcasted_iota(jnp.int32, (tm, tm), 1)
    tri = jnp.logical_and(ri >= ci, (ri // DELTA_CHUNK) == (ci // DELTA_CHUNK)).astype(BF16)
    kern = functools.partial(_inproj_kernel, n_heads=n_heads, n_delta_tiles=n_delta_tiles)
    row = lambda i, j: (i, 0)
    const = lambda i, j: (0, 0)
    once = dict(pipeline_mode=pl.Buffered(1))
    return pl.pallas_call(
        kern,
        grid=(tokens // tm, n_tiles),
        in_specs=[pl.BlockSpec((tm, d_model), row),
                  pl.BlockSpec((1, d_model), const),
                  pl.BlockSpec((d_model, width), lambda i, j: (0, jnp.minimum(j, n_delta_tiles - 1))),
                  pl.BlockSpec((d_model, width), lambda i, j: (0, jnp.maximum(j - n_delta_tiles, 0))),
                  pl.BlockSpec((d_model, LANES), lambda i, j: (0, n_delta_tiles * width // LANES), **once),
                  pl.BlockSpec((tm, tm), const, **once),
                  pl.BlockSpec((1, LANES), const),
                  pl.BlockSpec((1, LANES), const),
                  pl.BlockSpec((1, HEAD_DIM), const),
                  pl.BlockSpec((1, HEAD_DIM), const),
                  pl.BlockSpec((tm, HEAD_DIM), row),
                  pl.BlockSpec((tm, HEAD_DIM), row)],
        out_specs=[pl.BlockSpec((tm, width), lambda i, j: (i, j)),
                   pl.BlockSpec((tm, LANES), row)],
        out_shape=[jax.ShapeDtypeStruct((tokens, n_tiles * width), F32),
                   jax.ShapeDtypeStruct((tokens, LANES), F32)],
        scratch_shapes=[pltpu.VMEM((tm, d_model), BF16)],
        compiler_params=pltpu.CompilerParams(
            dimension_semantics=("parallel", "arbitrary"), vmem_limit_bytes=VMEM_LIMIT),
        name="inproj",
    )(x2, nw, w_all, w_attn, w_all, tri, alog, dtb, qnw, knw, cos_t, sin_t)


def _delta_kernel(q_ref, k_ref, v_ref, z_ref, gates_ref, cwq_ref, cwk_ref, cwv_ref, onw_ref,
                  o_ref, state_ref, carry_ref, xpad_ref, *, n_heads, heads_per_step):
    hg = pl.program_id(1)
    s_idx = pl.program_id(2)
    ts = q_ref.shape[0]
    c = DELTA_CHUNK
    n_chunks = ts // c
    hd = HEAD_DIM

    @pl.when(s_idx == 0)
    def _():
        state_ref[...] = jnp.zeros_like(state_ref)
        carry_ref[...] = jnp.zeros_like(carry_ref)

    def conv_silu(x_ref, cw_ref, slot):
        w = cw_ref[...]
        halves = []
        for hh in range(heads_per_step):
            cols = slice(hh * hd, (hh + 1) * hd)
            xpad_ref[hh, pl.ds(0, SUBLANES, stride=2), :] = carry_ref[slot, :, cols]
            xpad_ref[hh, pl.ds(2 * SUBLANES, ts, stride=2), :] = x_ref[:, cols]
            y = jnp.zeros((ts, hd), F32)
            for j in range(CONV_WIDTH):
                off = 2 * (SUBLANES - (CONV_WIDTH - 1) + j)
                y = y + xpad_ref[hh, pl.ds(off, ts, stride=2), :] * w[j:j + 1, cols]
            halves.append(_silu(y))
        carry_ref[slot] = x_ref[ts - SUBLANES:ts, :]
        return halves

    def l2n(t):
        return t * lax.rsqrt(jnp.sum(t * t, axis=-1, keepdims=True) + EPS)

    q_conv = conv_silu(q_ref, cwq_ref, 0)
    k_conv = conv_silu(k_ref, cwk_ref, 1)
    v_conv = conv_silu(v_ref, cwv_ref, 2)
    gt = gates_ref[...]
    lane = lax.broadcasted_iota(jnp.int32, gt.shape, 1)

    ri = lax.broadcasted_iota(jnp.int32, (c, c), 0)
    ci = lax.broadcasted_iota(jnp.int32, (c, c), 1)
    incl = ri >= ci
    strict = ri > ci
    eye = (ri == ci).astype(F32)
    onw = onw_ref[...]

    qs, ks, vs, betas, gcbs = [], [], [], [], []
    for hh in range(heads_per_step):
        cols = slice(hh * hd, (hh + 1) * hd)
        head = hg * heads_per_step + hh
        q_h = l2n(q_conv[hh]) * (hd ** -0.5)
        k_h = l2n(k_conv[hh])
        v_h = v_conv[hh]
        beta_h = jnp.broadcast_to(
            jnp.sum(jnp.where(lane == head, gt, 0.0), axis=-1, keepdims=True), (ts, c))
        gc_h = jnp.broadcast_to(
            jnp.sum(jnp.where(lane == head + n_heads, gt, 0.0), axis=-1, keepdims=True), (ts, c))
        for ch in range(n_chunks):
            r = slice(ch * c, (ch + 1) * c)
            qs.append(q_h[r])
            ks.append(k_h[r])
            vs.append(v_h[r])
            betas.append(beta_h[r])
            gcbs.append(gc_h[r])
    units = range(heads_per_step * n_chunks)

    kks = [_mm_nt(ks[n], ks[n]) for n in units]
    qks = [_mm_nt(qs[n], ks[n]) for n in units]
    decays = []
    for n in units:
        gcr = gcbs[n].T
        decays.append(jnp.where(incl, jnp.exp(jnp.where(incl, gcbs[n] - gcr, 0.0)), 0.0))
    lowers = [jnp.where(strict, betas[n] * kks[n] * decays[n], 0.0) for n in units]
    same2 = (ri // 2) == (ci // 2)
    invs = [eye - jnp.where(same2, lowers[n], 0.0) for n in units]
    b = 2
    while b < c:
        off = jnp.logical_and((ri // (2 * b)) == (ci // (2 * b)), (ri // b) != (ci // b))
        tmp = [_mm(jnp.where(off, lowers[n], 0.0), invs[n]) for n in units]
        invs = [invs[n] - _mm(invs[n], tmp[n]) for n in units]
        b *= 2
    egcs = [jnp.exp(gcbs[n]) for n in units]
    us = [_mm(invs[n], vs[n] * betas[n]) for n in units]
    ws = [_mm(invs[n], ks[n] * (betas[n] * egcs[n])) for n in units]
    qks = [qks[n] * decays[n] for n in units]
    g_lasts = [gcbs[n][c - 1:c, :] for n in units]
    kd_ts = [(ks[n] * jnp.exp(g_lasts[n] - gcbs[n])).T for n in units]
    s_mix = [_mm(kd_ts[n], ws[n]) for n in units]
    s_add = [_mm(kd_ts[n], us[n]) for n in units]
    q_eff = [qs[n] * egcs[n] - _mm(qks[n], ws[n]) for n in units]
    o_loc = [_mm(qks[n], us[n]) for n in units]

    states = [state_ref[hh] for hh in range(heads_per_step)]
    outs = {}
    for ch in range(n_chunks):
        for hh in range(heads_per_step):
            n = hh * n_chunks + ch
            outs[n] = _mm(q_eff[n], states[hh]) + o_loc[n]
            states[hh] = (states[hh] * jnp.exp(g_lasts[n]) - _mm(s_mix[n], states[hh])) + s_add[n]
    for hh in range(heads_per_step):
        state_ref[hh] = states[hh]

    for hh in range(heads_per_step):
        cols = slice(hh * hd, (hh + 1) * hd)
        for ch in range(n_chunks):
            r = slice(ch * c, (ch + 1) * c)
            o_c = outs[hh * n_chunks + ch]
            o_n = o_c * lax.rsqrt(jnp.mean(o_c * o_c, axis=-1, keepdims=True) + EPS) * onw
            o_ref[r, cols] = (o_n * _silu(z_ref[r, cols])).astype(o_ref.dtype)


def _delta(proj, gates, conv_w, onw, bsz, seq, n_heads):
    ts = 1024
    hb = 2
    s_tiles = seq // ts
    groups = n_heads // hb
    kern = functools.partial(_delta_kernel, n_heads=n_heads, heads_per_step=hb)

    def col(seg):
        return lambda b, g, s: (b * s_tiles + s, seg * groups + g)

    def cw(seg):
        return lambda b, g, s: (0, seg * groups + g)

    tile = (ts, hb * HEAD_DIM)
    return pl.pallas_call(
        kern,
        grid=(bsz, groups, s_tiles),
        in_specs=[pl.BlockSpec(tile, col(0)), pl.BlockSpec(tile, col(1)),
                  pl.BlockSpec(tile, col(2)), pl.BlockSpec(tile, col(3)),
                  pl.BlockSpec((ts, LANES), lambda b, g, s: (b * s_tiles + s, 0)),
                  pl.BlockSpec((CONV_WIDTH, hb * HEAD_DIM), cw(0)),
                  pl.BlockSpec((CONV_WIDTH, hb * HEAD_DIM), cw(1)),
                  pl.BlockSpec((CONV_WIDTH, hb * HEAD_DIM), cw(2)),
                  pl.BlockSpec((1, HEAD_DIM), lambda b, g, s: (0, 0))],
        out_specs=pl.BlockSpec(tile, lambda b, g, s: (b * s_tiles + s, g)),
        out_shape=jax.ShapeDtypeStruct((bsz * seq, n_heads * HEAD_DIM), BF16),
        scratch_shapes=[pltpu.VMEM((hb, HEAD_DIM, HEAD_DIM), F32),
                        pltpu.VMEM((3, SUBLANES, hb * HEAD_DIM), F32),
                        pltpu.VMEM((hb, 2 * (ts + SUBLANES), HEAD_DIM), F32)],
        compiler_params=pltpu.CompilerParams(
            dimension_semantics=("parallel", "parallel", "arbitrary"),
            vmem_limit_bytes=VMEM_LIMIT),
        name="delta_rule",
    )(proj, proj, proj, proj, gates, conv_w, conv_w, conv_w, onw)


def _attn_kernel(q_ref, k_ref, v_ref, onw_ref, o_ref, acc_ref, m_ref, l_ref):
    seq = q_ref.shape[0]
    scale = HEAD_DIM ** -0.5
    onw = onw_ref[...]
    order = sorted(DILATED_PATTERNS, key=lambda p: -p[1])

    for pi, (window, dil) in enumerate(order):
        first, last = pi == 0, pi == len(order) - 1
        blk = window // dil
        nb = seq // (blk * dil)
        qi = lax.broadcasted_iota(jnp.int32, (blk, blk), 0)
        ki = lax.broadcasted_iota(jnp.int32, (blk, blk), 1)

        def body(it, carry, first=first, last=last, dil=dil, blk=blk, nb=nb, qi=qi, ki=ki):
            loaded = []
            for u in range(ATTN_UNROLL):
                idx = it * ATTN_UNROLL + u
                r = idx // nb
                j = idx % nb
                start = j * (blk * dil) + r
                pstart = jnp.maximum(j - 1, 0) * (blk * dil) + r
                if dil == 1:
                    cur = pl.ds(pl.multiple_of(start, blk), blk)
                    prev = pl.ds(pl.multiple_of(pstart, blk), blk)
                else:
                    cur, prev = pl.ds(start, blk, stride=dil), pl.ds(pstart, blk, stride=dil)
                tiles = (q_ref[cur, :] * scale, k_ref[cur, :], k_ref[prev, :],
                         v_ref[cur, :], v_ref[prev, :])
                old = None if first else (m_ref[cur, :], l_ref[cur, :], acc_ref[cur, :])
                loaded.append((cur, j, tiles, old))

            scores = []
            for cur, j, (qb, kc, kp, vc, vp), old in loaded:
                s_cur = jnp.where(ki <= qi, _mm_nt(qb, kc), NEG_BIG)
                s_prev = jnp.where(jnp.logical_and(ki >= qi, j > 0), _mm_nt(qb, kp), NEG_BIG)
                scores.append((s_cur, s_prev))

            probs = []
            for (cur, j, tiles, old), (s_cur, s_prev) in zip(loaded, scores):
                m_new = jnp.max(jnp.maximum(s_cur, s_prev), axis=-1, keepdims=True)
                if old is not None:
                    m_new = jnp.maximum(old[0], m_new)
                p_cur = jnp.exp(s_cur - m_new)
                p_prev = jnp.exp(s_prev - m_new)
                l_new = jnp.sum(p_cur + p_prev, axis=-1, keepdims=True)
                probs.append((m_new, p_cur, p_prev, l_new))

            results = []
            for (cur, j, (qb, kc, kp, vc, vp), old), (m_new, p_cur, p_prev, l_new) in zip(loaded, probs):
                acc_new = _mm(p_cur, vc) + _mm(p_prev, vp)
                if old is not None:
                    alpha = jnp.exp(old[0] - m_new)
                    l_new = alpha * old[1] + l_new
                    acc_new = alpha * old[2] + acc_new
                results.append((cur, m_new, l_new, acc_new))

            for cur, m_new, l_new, acc_new in results:
                if last:
                    o = acc_new / l_new
                    o = o * lax.rsqrt(jnp.mean(o * o, axis=-1, keepdims=True) + EPS) * onw
                    o_ref[cur, :] = o.astype(o_ref.dtype)
                else:
                    m_ref[cur, :] = jnp.broadcast_to(m_new, (blk, HEAD_DIM))
                    l_ref[cur, :] = jnp.broadcast_to(l_new, (blk, HEAD_DIM))
                    acc_ref[cur, :] = acc_new
            return carry

        lax.fori_loop(0, dil * nb // ATTN_UNROLL, body, 0)


def _attention(proj, onw, bsz, seq, n_heads, col0):
    tile = (seq, HEAD_DIM)

    def col(seg):
        return lambda b, h: (b, col0 + seg * n_heads + h)

    return pl.pallas_call(
        _attn_kernel,
        grid=(bsz, n_heads),
        in_specs=[pl.BlockSpec(tile, col(0)), pl.BlockSpec(tile, col(1)),
                  pl.BlockSpec(tile, col(2)),
                  pl.BlockSpec((1, HEAD_DIM), lambda b, h: (0, 0))],
        out_specs=pl.BlockSpec(tile, lambda b, h: (b, h)),
        out_shape=jax.ShapeDtypeStruct((bsz * seq, n_heads * HEAD_DIM), BF16),
        scratch_shapes=[pltpu.VMEM(tile, F32)] * 3,
        compiler_params=pltpu.CompilerParams(
            dimension_semantics=("parallel", "parallel"), vmem_limit_bytes=VMEM_LIMIT),
        name="dilated_attention",
    )(proj, proj, proj, onw)


def _outproj_kernel(oa_ref, ob_ref, wa_ref, wb_ref, x_ref, fnw_ref, x1_ref, h2_ref):
    acc = (jnp.dot(oa_ref[...], wa_ref[...], preferred_element_type=F32)
           + jnp.dot(ob_ref[...], wb_ref[...], preferred_element_type=F32))
    x1 = x_ref[...] + acc
    x1_ref[...] = x1
    h2 = x1 * lax.rsqrt(jnp.mean(x1 * x1, axis=-1, keepdims=True) + EPS) * fnw_ref[...]
    h2_ref[...] = h2.astype(h2_ref.dtype)


def _outproj(o_a, o_b, w_out, x2, fnw):
    tokens, d_model = x2.shape
    wa_rows = o_a.shape[1]
    tm = 512
    row = lambda i: (i, 0)
    return pl.pallas_call(
        _outproj_kernel,
        grid=(tokens // tm,),
        in_specs=[pl.BlockSpec((tm, wa_rows), row),
                  pl.BlockSpec((tm, o_b.shape[1]), row),
                  pl.BlockSpec((wa_rows, d_model), lambda i: (0, 0)),
                  pl.BlockSpec((o_b.shape[1], d_model), lambda i: (1, 0)),
                  pl.BlockSpec((tm, d_model), row),
                  pl.BlockSpec((1, d_model), lambda i: (0, 0))],
        out_specs=[pl.BlockSpec((tm, d_model), row), pl.BlockSpec((tm, d_model), row)],
        out_shape=[jax.ShapeDtypeStruct((tokens, d_model), F32),
                   jax.ShapeDtypeStruct((tokens, d_model), BF16)],
        compiler_params=pltpu.CompilerParams(
            dimension_semantics=("parallel",), vmem_limit_bytes=VMEM_LIMIT),
        name="outproj",
    )(o_a, o_b, w_out, w_out, x2, fnw)


def _ffn_kernel(h_ref, wg_ref, wu_ref, wd_ref, x1_ref, o_ref):
    @pl.when(pl.program_id(1) == 0)
    def _():
        o_ref[...] = x1_ref[...]

    h = h_ref[...]
    g = jnp.dot(h, wg_ref[...], preferred_element_type=F32)
    u = jnp.dot(h, wu_ref[...], preferred_element_type=F32)
    a = (_silu(g) * u).astype(BF16)
    o_ref[...] += jnp.dot(a, wd_ref[...], preferred_element_type=F32)


def _ffn(h2, w_gate_up, w_down, x1):
    tokens, d_model = x1.shape
    d_ff = w_down.shape[0]
    tm, tf = 1024, 512
    nf = d_ff // tf
    row = lambda i, f: (i, 0)
    return pl.pallas_call(
        _ffn_kernel,
        grid=(tokens // tm, nf),
        in_specs=[pl.BlockSpec((tm, d_model), row),
                  pl.BlockSpec((d_model, tf), lambda i, f: (0, f)),
                  pl.BlockSpec((d_model, tf), lambda i, f: (0, nf + f)),
                  pl.BlockSpec((tf, d_model), lambda i, f: (f, 0)),
                  pl.BlockSpec((tm, d_model), row)],
        out_specs=pl.BlockSpec((tm, d_model), row),
        out_shape=jax.ShapeDtypeStruct((tokens, d_model), F32),
        compiler_params=pltpu.CompilerParams(
            dimension_semantics=("parallel", "arbitrary"), vmem_limit_bytes=VMEM_LIMIT),
        name="swiglu",
    )(h2, w_gate_up, w_gate_up, w_down, x1)


def _pad_lanes(v, offset):
    out = jnp.zeros((1, LANES), F32)
    return lax.dynamic_update_slice(out, v.astype(F32)[None, :], (0, offset))


def kernel(x, positions, attn_norm_w, w_in, conv_w, a_log, dt_bias, delta_out_norm_w, q_norm_w,
           k_norm_w, attn_out_norm_w, w_out, ffn_norm_w, w_gate_up, w_down):
    bsz, seq, d_model = x.shape
    depth = w_in.shape[0]
    n_heads = a_log.shape[1]
    width = n_heads * HEAD_DIM
    gate_lo, gate_hi = 4 * width, 4 * width + 2 * n_heads
    assert 2 * n_heads <= LANES and w_in.shape[2] == gate_hi + 3 * width

    cos_t, sin_t = _rope_tables(positions)
    x2 = x.reshape(bsz * seq, d_model)
    for l in range(depth):
        w_all = w_in[l].astype(BF16)
        proj, gates = _inproj(
            x2, attn_norm_w[l][None, :], w_all, w_all[:, gate_hi:],
            _pad_lanes(a_log[l], n_heads), _pad_lanes(dt_bias[l], n_heads),
            q_norm_w[l][None, :], k_norm_w[l][None, :], cos_t, sin_t, n_heads)
        o_a = _delta(proj, gates, conv_w[l], delta_out_norm_w[l][None, :], bsz, seq, n_heads)
        o_b = _attention(proj, attn_out_norm_w[l][None, :], bsz, seq, n_heads, 4 * n_heads)
        x1, h2 = _outproj(o_a, o_b, w_out[l].astype(BF16), x2, ffn_norm_w[l][None, :])
        x2 = _ffn(h2, w_gate_up[l].astype(BF16), w_down[l].astype(BF16), x1)
    return x2.reshape(bsz, seq, d_model)
```

```python
import functools

import jax
import jax.numpy as jnp
from jax import lax
from jax.experimental import pallas as pl
from jax.experimental.pallas import tpu as pltpu

HEAD_DIM = 128
CONV_WIDTH = 4
DILATED_PATTERNS = ((128, 1), (512, 4), (2048, 16))
ROPE_THETA = 10000.0
EPS = 1e-6

LANES = 128
SUBLANES = 8
DELTA_CHUNK = 128
NEG_BIG = -1e30
ATTN_UNROLL = 8

F32 = jnp.float32
BF16 = jnp.bfloat16

VMEM_LIMIT = 60 * 1024 * 1024


def _mm(a, b):
    return jnp.dot(a.astype(BF16), b.astype(BF16), preferred_element_type=F32)


def _mm_nt(a, b):
    return lax.dot_general(a.astype(BF16), b.astype(BF16), (((1,), (1,)), ((), ())),
                           preferred_element_type=F32)


def _split2(x):
    hi = x.astype(BF16)
    lo = (x - hi.astype(F32)).astype(BF16)
    return jnp.concatenate([hi, lo], axis=1)


def _silu(x):
    return x * jax.nn.sigmoid(x)


def _rope_table_kernel(pos_ref, invf_ref, cos_ref, sin_ref):
    ang = pos_ref[...].astype(F32) * invf_ref[...]
    lane = lax.broadcasted_iota(jnp.int32, ang.shape, 1)
    cos_ref[...] = jnp.cos(ang)
    s = jnp.sin(ang)
    sin_ref[...] = jnp.where(lane < HEAD_DIM // 2, -s, s)


def _rope_tables(positions):
    tokens = positions.size
    tm = 1024
    half = HEAD_DIM // 2
    inv_freq = ROPE_THETA ** (-jnp.arange(half, dtype=F32) / half)
    invf = jnp.concatenate([inv_freq, inv_freq])[None, :]
    return pl.pallas_call(
        _rope_table_kernel,
        grid=(tokens // tm,),
        in_specs=[pl.BlockSpec((tm, 1), lambda i: (i, 0)),
                  pl.BlockSpec((1, HEAD_DIM), lambda i: (0, 0))],
        out_specs=[pl.BlockSpec((tm, HEAD_DIM), lambda i: (i, 0))] * 2,
        out_shape=[jax.ShapeDtypeStruct((tokens, HEAD_DIM), F32)] * 2,
        name="rope_tables",
    )(positions.reshape(tokens, 1), invf)


def _inproj_kernel(x_ref, nw_ref, wd_ref, wa_ref, wg_ref, tri_ref, alog_ref, dtb_ref, qnw_ref,
                   knw_ref, cos_ref, sin_ref, out_ref, gates_ref, hn_ref, *, n_heads, n_delta_tiles):
    j = pl.program_id(1)

    @pl.when(j == 0)
    def _():
        x = x_ref[...]
        y = x * lax.rsqrt(jnp.mean(x * x, axis=-1, keepdims=True) + EPS) * nw_ref[...]
        hn = y.astype(BF16)
        hn_ref[...] = hn
        gl = jnp.dot(hn, wg_ref[...], preferred_element_type=F32)
        lane = lax.broadcasted_iota(jnp.int32, gl.shape, 1)
        beta = jax.nn.sigmoid(gl)
        t = gl + dtb_ref[...]
        softplus = jnp.maximum(t, 0.0) + jnp.log1p(jnp.exp(-jnp.abs(t)))
        g = -jnp.exp(alog_ref[...]) * softplus
        g_hi = g.astype(BF16)
        rest = g - g_hi.astype(F32)
        g_mid = rest.astype(BF16)
        g_lo = (rest - g_mid.astype(F32)).astype(BF16)
        parts = jnp.dot(tri_ref[...], jnp.concatenate([g_hi, g_mid, g_lo], axis=1),
                        preferred_element_type=F32)
        gc = parts[:, :LANES] + parts[:, LANES:2 * LANES] + parts[:, 2 * LANES:]
        gates_ref[...] = jnp.where(lane < n_heads, beta, gc)

    @pl.when(j < n_delta_tiles)
    def _():
        out_ref[...] = jnp.dot(hn_ref[...], wd_ref[...],
                               preferred_element_type=F32).astype(out_ref.dtype)

    @pl.when(j == n_delta_tiles + 2)
    def _():
        out_ref[...] = jnp.dot(hn_ref[...], wa_ref[...],
                               preferred_element_type=F32).astype(out_ref.dtype)

    @pl.when(jnp.logical_or(j == n_delta_tiles, j == n_delta_tiles + 1))
    def _():
        acc = jnp.dot(hn_ref[...], wa_ref[...], preferred_element_type=F32)
        nw = jnp.where(j == n_delta_tiles, qnw_ref[...], knw_ref[...])
        out_scale = jnp.where(j == n_delta_tiles, HEAD_DIM ** -0.5, 1.0)
        cos = cos_ref[...]
        sin = sin_ref[...]
        pi = lax.broadcasted_iota(jnp.int32, (2 * HEAD_DIM, HEAD_DIM), 0)
        pj = lax.broadcasted_iota(jnp.int32, (2 * HEAD_DIM, HEAD_DIM), 1)
        ones2 = jnp.ones((2 * HEAD_DIM, HEAD_DIM), BF16)
        rot2 = ((pi % HEAD_DIM) == ((pj + HEAD_DIM // 2) % HEAD_DIM)).astype(BF16)
        for h in range(n_heads):
            xh = acc[:, h * HEAD_DIM:(h + 1) * HEAD_DIM]
            mean_sq = jnp.dot(_split2(xh * xh), ones2, preferred_element_type=F32) * (1.0 / HEAD_DIM)
            yh = xh * lax.rsqrt(mean_sq + EPS) * nw
            yh_rot = jnp.dot(_split2(yh), rot2, preferred_element_type=F32)
            roped = (yh * cos + yh_rot * sin) * out_scale
            out_ref[:, h * HEAD_DIM:(h + 1) * HEAD_DIM] = roped.astype(out_ref.dtype)


def _inproj(x2, nw, w_all, w_attn, alog, dtb, qnw, knw, cos_t, sin_t, n_heads):
    tokens, d_model = x2.shape
    width = n_heads * HEAD_DIM
    n_delta_tiles = 4
    n_tiles = n_delta_tiles + w_attn.shape[1] // width
    tm = 1024
    ri = lax.broadcasted_iota(jnp.int32, (tm, tm), 0)
    ci = lax.broadcasted_iota(jnp.int32, (tm, tm), 1)
    tri = jnp.logical_and(ri >= ci, (ri // DELTA_CHUNK) == (ci // DELTA_CHUNK)).astype(BF16)
    kern = functools.partial(_inproj_kernel, n_heads=n_heads, n_delta_tiles=n_delta_tiles)
    row = lambda i, j: (i, 0)
    const = lambda i, j: (0, 0)
    once = dict(pipeline_mode=pl.Buffered(1))
    return pl.pallas_call(
        kern,
        grid=(tokens // tm, n_tiles),
        in_specs=[pl.BlockSpec((tm, d_model), row),
                  pl.BlockSpec((1, d_model), const),
                  pl.BlockSpec((d_model, width), lambda i, j: (0, jnp.minimum(j, n_delta_tiles - 1))),
                  pl.BlockSpec((d_model, width), lambda i, j: (0, jnp.maximum(j - n_delta_tiles, 0))),
                  pl.BlockSpec((d_model, LANES), lambda i, j: (0, n_delta_tiles * width // LANES), **once),
                  pl.BlockSpec((tm, tm), const, **once),
                  pl.BlockSpec((1, LANES), const),
                  pl.BlockSpec((1, LANES), const),
                  pl.BlockSpec((1, HEAD_DIM), const),
                  pl.BlockSpec((1, HEAD_DIM), const),
                  pl.BlockSpec((tm, HEAD_DIM), row),
                  pl.BlockSpec((tm, HEAD_DIM), row)],
        out_specs=[pl.BlockSpec((tm, width), lambda i, j: (i, j)),
                   pl.BlockSpec((tm, LANES), row)],
        out_shape=[jax.ShapeDtypeStruct((tokens, n_tiles * width), BF16),
                   jax.ShapeDtypeStruct((tokens, LANES), F32)],
        scratch_shapes=[pltpu.VMEM((tm, d_model), BF16)],
        compiler_params=pltpu.CompilerParams(
            dimension_semantics=("parallel", "arbitrary"), vmem_limit_bytes=VMEM_LIMIT),
        name="inproj",
    )(x2, nw, w_all, w_attn, w_all, tri, alog, dtb, qnw, knw, cos_t, sin_t)


def _delta_kernel(q_ref, k_ref, v_ref, z_ref, gates_ref, cwq_ref, cwk_ref, cwv_ref, onw_ref,
                  o_ref, state_ref, carry_ref, xpad_ref, *, n_heads, heads_per_step):
    hg = pl.program_id(1)
    s_idx = pl.program_id(2)
    ts = q_ref.shape[0]
    c = DELTA_CHUNK
    n_chunks = ts // c
    hd = HEAD_DIM

    @pl.when(s_idx == 0)
    def _():
        state_ref[...] = jnp.zeros_like(state_ref)
        carry_ref[...] = jnp.zeros_like(carry_ref)

    def conv_silu(x_ref, cw_ref, slot):
        w = cw_ref[...]
        halves = []
        for hh in range(heads_per_step):
            cols = slice(hh * hd, (hh + 1) * hd)
            xpad_ref[hh, pl.ds(0, SUBLANES, stride=2), :] = carry_ref[slot, :, cols]
            xpad_ref[hh, pl.ds(2 * SUBLANES, ts, stride=2), :] = x_ref[:, cols].astype(F32)
            y = jnp.zeros((ts, hd), F32)
            for j in range(CONV_WIDTH):
                off = 2 * (SUBLANES - (CONV_WIDTH - 1) + j)
                y = y + xpad_ref[hh, pl.ds(off, ts, stride=2), :] * w[j:j + 1, cols]
            halves.append(_silu(y))
            carry_ref[slot, :, cols] = xpad_ref[hh, pl.ds(2 * ts, SUBLANES, stride=2), :]
        return halves

    def l2n(t):
        return t * lax.rsqrt(jnp.sum(t * t, axis=-1, keepdims=True) + EPS)

    q_conv = conv_silu(q_ref, cwq_ref, 0)
    k_conv = conv_silu(k_ref, cwk_ref, 1)
    v_conv = conv_silu(v_ref, cwv_ref, 2)
    gt = gates_ref[...]
    lane = lax.broadcasted_iota(jnp.int32, gt.shape, 1)

    ri = lax.broadcasted_iota(jnp.int32, (c, c), 0)
    ci = lax.broadcasted_iota(jnp.int32, (c, c), 1)
    incl = ri >= ci
    strict = ri > ci
    eye = (ri == ci).astype(F32)
    onw = onw_ref[...]

    qs, ks, vs, betas, gcbs = [], [], [], [], []
    for hh in range(heads_per_step):
        cols = slice(hh * hd, (hh + 1) * hd)
        head = hg * heads_per_step + hh
        q_h = l2n(q_conv[hh]) * (hd ** -0.5)
        k_h = l2n(k_conv[hh])
        v_h = v_conv[hh]
        beta_h = jnp.broadcast_to(
            jnp.sum(jnp.where(lane == head, gt, 0.0), axis=-1, keepdims=True), (ts, c))
        gc_h = jnp.broadcast_to(
            jnp.sum(jnp.where(lane == head + n_heads, gt, 0.0), axis=-1, keepdims=True), (ts, c))
        for ch in range(n_chunks):
            r = slice(ch * c, (ch + 1) * c)
            qs.append(q_h[r])
            ks.append(k_h[r])
            vs.append(v_h[r])
            betas.append(beta_h[r])
            gcbs.append(gc_h[r])
    units = range(heads_per_step * n_chunks)

    kks = [_mm_nt(ks[n], ks[n]) for n in units]
    qks = [_mm_nt(qs[n], ks[n]) for n in units]
    decays = []
    for n in units:
        gcr = gcbs[n].T
        decays.append(jnp.where(incl, jnp.exp(jnp.where(incl, gcbs[n] - gcr, 0.0)), 0.0))
    lowers = [jnp.where(strict, betas[n] * kks[n] * decays[n], 0.0) for n in units]
    same2 = (ri // 2) == (ci // 2)
    invs = [eye - jnp.where(same2, lowers[n], 0.0) for n in units]
    b = 2
    while b < c:
        off = jnp.logical_and((ri // (2 * b)) == (ci // (2 * b)), (ri // b) != (ci // b))
        tmp = [_mm(jnp.where(off, lowers[n], 0.0), invs[n]) for n in units]
        invs = [invs[n] - _mm(invs[n], tmp[n]) for n in units]
        b *= 2
    egcs = [jnp.exp(gcbs[n]) for n in units]
    us = [_mm(invs[n], vs[n] * betas[n]) for n in units]
    ws = [_mm(invs[n], ks[n] * (betas[n] * egcs[n])) for n in units]
    qks = [qks[n] * decays[n] for n in units]
    g_lasts = [gcbs[n][c - 1:c, :] for n in units]
    kd_ts = [(ks[n] * jnp.exp(g_lasts[n] - gcbs[n])).T for n in units]
    s_mix = [_mm(kd_ts[n], ws[n]) for n in units]
    s_add = [_mm(kd_ts[n], us[n]) for n in units]
    q_eff = [qs[n] * egcs[n] - _mm(qks[n], ws[n]) for n in units]
    o_loc = [_mm(qks[n], us[n]) for n in units]

    states = [state_ref[hh] for hh in range(heads_per_step)]
    outs = {}
    for ch in range(n_chunks):
        for hh in range(heads_per_step):
            n = hh * n_chunks + ch
            outs[n] = _mm(q_eff[n], states[hh]) + o_loc[n]
            states[hh] = (states[hh] * jnp.exp(g_lasts[n]) - _mm(s_mix[n], states[hh])) + s_add[n]
    for hh in range(heads_per_step):
        state_ref[hh] = states[hh]

    for hh in range(heads_per_step):
        cols = slice(hh * hd, (hh + 1) * hd)
        for ch in range(n_chunks):
            r = slice(ch * c, (ch + 1) * c)
            o_c = outs[hh * n_chunks + ch]
            o_n = o_c * lax.rsqrt(jnp.mean(o_c * o_c, axis=-1, keepdims=True) + EPS) * onw
            o_ref[r, cols] = (o_n * _silu(z_ref[r, cols].astype(F32))).astype(o_ref.dtype)


def _delta(proj, gates, conv_w, onw, bsz, seq, n_heads):
    ts = 1024
    hb = 2
    s_tiles = seq // ts
    groups = n_heads // hb
    kern = functools.partial(_delta_kernel, n_heads=n_heads, heads_per_step=hb)

    def col(seg):
        return lambda b, g, s: (b * s_tiles + s, seg * groups + g)

    def cw(seg):
        return lambda b, g, s: (0, seg * groups + g)

    tile = (ts, hb * HEAD_DIM)
    return pl.pallas_call(
        kern,
        grid=(bsz, groups, s_tiles),
        in_specs=[pl.BlockSpec(tile, col(0)), pl.BlockSpec(tile, col(1)),
                  pl.BlockSpec(tile, col(2)), pl.BlockSpec(tile, col(3)),
                  pl.BlockSpec((ts, LANES), lambda b, g, s: (b * s_tiles + s, 0)),
                  pl.BlockSpec((CONV_WIDTH, hb * HEAD_DIM), cw(0)),
                  pl.BlockSpec((CONV_WIDTH, hb * HEAD_DIM), cw(1)),
                  pl.BlockSpec((CONV_WIDTH, hb * HEAD_DIM), cw(2)),
                  pl.BlockSpec((1, HEAD_DIM), lambda b, g, s: (0, 0))],
        out_specs=pl.BlockSpec(tile, lambda b, g, s: (b * s_tiles + s, g)),
        out_shape=jax.ShapeDtypeStruct((bsz * seq, n_heads * HEAD_DIM), BF16),
        scratch_shapes=[pltpu.VMEM((hb, HEAD_DIM, HEAD_DIM), F32),
                        pltpu.VMEM((3, SUBLANES, hb * HEAD_DIM), F32),
                        pltpu.VMEM((hb, 2 * (ts + SUBLANES), HEAD_DIM), F32)],
        compiler_params=pltpu.CompilerParams(
            dimension_semantics=("parallel", "parallel", "arbitrary"),
            vmem_limit_bytes=VMEM_LIMIT),
        name="delta_rule",
    )(proj, proj, proj, proj, gates, conv_w, conv_w, conv_w, onw)


def _attn_kernel(q_ref, k_ref, v_ref, onw_ref, o_ref, qf_ref, kf_ref, vf_ref, acc_ref, m_ref, l_ref):
    seq = q_ref.shape[0]
    hd = HEAD_DIM
    onw = onw_ref[...]
    order = sorted(DILATED_PATTERNS, key=lambda p: -p[1])
    qf_ref[...] = q_ref[...].astype(F32)
    kf_ref[...] = k_ref[...].astype(F32)
    vf_ref[...] = v_ref[...].astype(F32)

    for pi, (window, dil) in enumerate(order):
        first, last = pi == 0, pi == len(order) - 1
        blk = window // dil
        nb = seq // (blk * dil)
        run_len = min(nb, ATTN_UNROLL)
        runs_per_iter = ATTN_UNROLL // run_len
        runs_per_residue = nb // run_len
        qi = lax.broadcasted_iota(jnp.int32, (blk, blk), 0)
        ki = lax.broadcasted_iota(jnp.int32, (blk, blk), 1)
        cur_mask = ki <= qi
        prev_mask = ki >= qi
        ones_cols = jnp.ones((blk, hd), BF16)

        def body(it, carry, first=first, last=last, dil=dil, blk=blk, run_len=run_len,
                 runs_per_iter=runs_per_iter, runs_per_residue=runs_per_residue,
                 cur_mask=cur_mask, prev_mask=prev_mask, ones_cols=ones_cols):
            def rows(start):
                if dil == 1:
                    return pl.ds(pl.multiple_of(start, blk), blk)
                return pl.ds(start, blk, stride=dil)

            def tile(f32_ref, bf16_ref, start):
                if dil == 1:
                    return bf16_ref[rows(start), :]
                return f32_ref[rows(start), :].astype(BF16)

            blocks = []
            for rn in range(runs_per_iter):
                run = it * runs_per_iter + rn
                if runs_per_residue == 1:
                    r, j0 = run, 0
                    prev_k = prev_v = prev_ok = None
                else:
                    r = run // runs_per_residue
                    j0 = (run % runs_per_residue) * run_len
                    pstart = jnp.maximum(j0 - 1, 0) * (blk * dil) + r
                    prev_k, prev_v = tile(kf_ref, k_ref, pstart), tile(vf_ref, v_ref, pstart)
                    prev_ok = j0 > 0
                for jj in range(run_len):
                    start = (j0 + jj) * (blk * dil) + r
                    cur = rows(start)
                    k_t, v_t = tile(kf_ref, k_ref, start), tile(vf_ref, v_ref, start)
                    old = None if first else (m_ref[cur, :], l_ref[cur, :], acc_ref[cur, :])
                    blocks.append((cur, tile(qf_ref, q_ref, start), k_t, v_t, prev_k, prev_v,
                                   prev_ok, old))
                    prev_k, prev_v, prev_ok = k_t, v_t, None

            scores = []
            for cur, q_t, k_t, v_t, prev_k, prev_v, prev_ok, old in blocks:
                if prev_k is None:
                    scores.append((jnp.where(cur_mask, _mm_nt(q_t, k_t), NEG_BIG), None))
                else:
                    s2 = _mm_nt(q_t, jnp.concatenate([k_t, prev_k], axis=0))
                    pm = prev_mask if prev_ok is None else jnp.logical_and(prev_mask, prev_ok)
                    scores.append((jnp.where(cur_mask, s2[:, :blk], NEG_BIG),
                                   jnp.where(pm, s2[:, blk:], NEG_BIG)))

            probs = []
            for (cur, q_t, k_t, v_t, prev_k, prev_v, prev_ok, old), (s_cur, s_prev) in zip(blocks, scores):
                s_max = s_cur if s_prev is None else jnp.maximum(s_cur, s_prev)
                m_new = jnp.max(s_max, axis=-1, keepdims=True)
                if old is not None:
                    m_new = jnp.maximum(old[0], m_new)
                p_cur = jnp.exp(s_cur - m_new).astype(BF16)
                p_prev = None if s_prev is None else jnp.exp(s_prev - m_new).astype(BF16)
                probs.append((m_new, p_cur, p_prev))

            results = []
            for (cur, q_t, k_t, v_t, prev_k, prev_v, prev_ok, old), (m_new, p_cur, p_prev) in zip(blocks, probs):
                v_ext = jnp.concatenate([v_t, ones_cols], axis=1)
                if p_prev is None:
                    pv = jnp.dot(p_cur, v_ext, preferred_element_type=F32)
                else:
                    pv = jnp.dot(jnp.concatenate([p_cur, p_prev], axis=1),
                                 jnp.concatenate(
                                     [v_ext, jnp.concatenate([prev_v, ones_cols], axis=1)], axis=0),
                                 preferred_element_type=F32)
                acc_new, l_new = pv[:, :hd], pv[:, hd:]
                if old is not None:
                    alpha = jnp.exp(old[0] - m_new)
                    l_new = alpha * old[1] + l_new
                    acc_new = alpha * old[2] + acc_new
                results.append((cur, m_new, l_new, acc_new))

            for cur, m_new, l_new, acc_new in results:
                if last:
                    o = acc_new / l_new
                    o = o * lax.rsqrt(jnp.mean(o * o, axis=-1, keepdims=True) + EPS) * onw
                    o_ref[cur, :] = o.astype(o_ref.dtype)
                else:
                    m_ref[cur, :] = jnp.broadcast_to(m_new, (blk, hd))
                    l_ref[cur, :] = l_new
                    acc_ref[cur, :] = acc_new
            return carry

        lax.fori_loop(0, dil * nb // ATTN_UNROLL, body, 0)


def _attention(proj, onw, bsz, seq, n_heads, col0):
    tile = (seq, HEAD_DIM)

    def col(seg):
        return lambda b, h: (b, col0 + seg * n_heads + h)

    return pl.pallas_call(
        _attn_kernel,
        grid=(bsz, n_heads),
        in_specs=[pl.BlockSpec(tile, col(0)), pl.BlockSpec(tile, col(1)),
                  pl.BlockSpec(tile, col(2)),
                  pl.BlockSpec((1, HEAD_DIM), lambda b, h: (0, 0))],
        out_specs=pl.BlockSpec(tile, lambda b, h: (b, h)),
        out_shape=jax.ShapeDtypeStruct((bsz * seq, n_heads * HEAD_DIM), BF16),
        scratch_shapes=[pltpu.VMEM(tile, F32)] * 6,
        compiler_params=pltpu.CompilerParams(
            dimension_semantics=("parallel", "parallel"), vmem_limit_bytes=VMEM_LIMIT),
        name="dilated_attention",
    )(proj, proj, proj, onw)


def _outproj_kernel(oa_ref, ob_ref, wa_ref, wb_ref, x_ref, fnw_ref, x1_ref, h2_ref):
    acc = (jnp.dot(oa_ref[...], wa_ref[...], preferred_element_type=F32)
           + jnp.dot(ob_ref[...], wb_ref[...], preferred_element_type=F32))
    x1 = x_ref[...] + acc
    x1_ref[...] = x1
    h2 = x1 * lax.rsqrt(jnp.mean(x1 * x1, axis=-1, keepdims=True) + EPS) * fnw_ref[...]
    h2_ref[...] = h2.astype(h2_ref.dtype)


def _outproj(o_a, o_b, w_out, x2, fnw):
    tokens, d_model = x2.shape
    wa_rows = o_a.shape[1]
    tm = 512
    row = lambda i: (i, 0)
    return pl.pallas_call(
        _outproj_kernel,
        grid=(tokens // tm,),
        in_specs=[pl.BlockSpec((tm, wa_rows), row),
                  pl.BlockSpec((tm, o_b.shape[1]), row),
                  pl.BlockSpec((wa_rows, d_model), lambda i: (0, 0)),
                  pl.BlockSpec((o_b.shape[1], d_model), lambda i: (1, 0)),
                  pl.BlockSpec((tm, d_model), row),
                  pl.BlockSpec((1, d_model), lambda i: (0, 0))],
        out_specs=[pl.BlockSpec((tm, d_model), row), pl.BlockSpec((tm, d_model), row)],
        out_shape=[jax.ShapeDtypeStruct((tokens, d_model), F32),
                   jax.ShapeDtypeStruct((tokens, d_model), BF16)],
        compiler_params=pltpu.CompilerParams(
            dimension_semantics=("parallel",), vmem_limit_bytes=VMEM_LIMIT),
        name="outproj",
    )(o_a, o_b, w_out, w_out, x2, fnw)


def _ffn_kernel(h_ref, wg_ref, wu_ref, wd_ref, x1_ref, o_ref):
    @pl.when(pl.program_id(1) == 0)
    def _():
        o_ref[...] = x1_ref[...]

    h = h_ref[...]
    g = jnp.dot(h, wg_ref[...], preferred_element_type=F32)
    u = jnp.dot(h, wu_ref[...], preferred_element_type=F32)
    a = (_silu(g) * u).astype(BF16)
    o_ref[...] += jnp.dot(a, wd_ref[...], preferred_element_type=F32)


def _ffn(h2, w_gate_up, w_down, x1):
    tokens, d_model = x1.shape
    d_ff = w_down.shape[0]
    tm, tf = 1024, 512
    nf = d_ff // tf
    row = lambda i, f: (i, 0)
    return pl.pallas_call(
        _ffn_kernel,
        grid=(tokens // tm, nf),
        in_specs=[pl.BlockSpec((tm, d_model), row),
                  pl.BlockSpec((d_model, tf), lambda i, f: (0, f)),
                  pl.BlockSpec((d_model, tf), lambda i, f: (0, nf + f)),
                  pl.BlockSpec((tf, d_model), lambda i, f: (f, 0)),
                  pl.BlockSpec((tm, d_model), row)],
        out_specs=pl.BlockSpec((tm, d_model), row),
        out_shape=jax.ShapeDtypeStruct((tokens, d_model), F32),
        compiler_params=pltpu.CompilerParams(
            dimension_semantics=("parallel", "arbitrary"), vmem_limit_bytes=VMEM_LIMIT),
        name="swiglu",
    )(h2, w_gate_up, w_gate_up, w_down, x1)


def _pad_lanes(v, offset):
    out = jnp.zeros((1, LANES), F32)
    return lax.dynamic_update_slice(out, v.astype(F32)[None, :], (0, offset))


def kernel(x, positions, attn_norm_w, w_in, conv_w, a_log, dt_bias, delta_out_norm_w, q_norm_w,
           k_norm_w, attn_out_norm_w, w_out, ffn_norm_w, w_gate_up, w_down):
    bsz, seq, d_model = x.shape
    depth = w_in.shape[0]
    n_heads = a_log.shape[1]
    width = n_heads * HEAD_DIM
    gate_lo, gate_hi = 4 * width, 4 * width + 2 * n_heads
    assert 2 * n_heads <= LANES and w_in.shape[2] == gate_hi + 3 * width

    cos_t, sin_t = _rope_tables(positions)
    x2 = x.reshape(bsz * seq, d_model)
    for l in range(depth):
        w_all = w_in[l].astype(BF16)
        proj, gates = _inproj(
            x2, attn_norm_w[l][None, :], w_all, w_all[:, gate_hi:],
            _pad_lanes(a_log[l], n_heads), _pad_lanes(dt_bias[l], n_heads),
            q_norm_w[l][None, :], k_norm_w[l][None, :], cos_t, sin_t, n_heads)
        o_a = _delta(proj, gates, conv_w[l], delta_out_norm_w[l][None, :], bsz, seq, n_heads)
        o_b = _attention(proj, attn_out_norm_w[l][None, :], bsz, seq, n_heads, 4 * n_heads)
        x1, h2 = _outproj(o_a, o_b, w_out[l].astype(BF16), x2, ffn_norm_w[l][None, :])
        x2 = _ffn(h2, w_gate_up[l].astype(BF16), w_down[l].astype(BF16), x1)
    return x2.reshape(bsz, seq, d_model)
```

```python
import functools

import jax
import jax.numpy as jnp
from jax import lax
from jax.experimental import pallas as pl
from jax.experimental.pallas import tpu as pltpu

HEAD_DIM = 128
CONV_WIDTH = 4
DILATED_PATTERNS = ((128, 1), (512, 4), (2048, 16))
ROPE_THETA = 10000.0
EPS = 1e-6

LANES = 128
SUBLANES = 8
DELTA_CHUNK = 128
NEG_BIG = -1e30
ATTN_UNROLL = 8

F32 = jnp.float32
BF16 = jnp.bfloat16

VMEM_LIMIT = 60 * 1024 * 1024


def _mm(a, b):
    return jnp.dot(a.astype(BF16), b.astype(BF16), preferred_element_type=F32)


def _mm_nt(a, b):
    return lax.dot_general(a.astype(BF16), b.astype(BF16), (((1,), (1,)), ((), ())),
                           preferred_element_type=F32)


def _split3(x):
    hi = x.astype(BF16)
    rest = x - hi.astype(F32)
    mid = rest.astype(BF16)
    lo = (rest - mid.astype(F32)).astype(BF16)
    return jnp.concatenate([hi, mid, lo], axis=1)


def _split2(x):
    hi = x.astype(BF16)
    lo = (x - hi.astype(F32)).astype(BF16)
    return jnp.concatenate([hi, lo], axis=1)


def _silu(x):
    return x * jax.nn.sigmoid(x)


def _rope_table_kernel(pos_ref, invf_ref, cos_ref, sin_ref):
    ang = pos_ref[...].astype(F32) * invf_ref[...]
    lane = lax.broadcasted_iota(jnp.int32, ang.shape, 1)
    cos_ref[...] = jnp.cos(ang)
    s = jnp.sin(ang)
    sin_ref[...] = jnp.where(lane < HEAD_DIM // 2, -s, s)


def _rope_tables(positions):
    tokens = positions.size
    tm = 1024
    half = HEAD_DIM // 2
    inv_freq = ROPE_THETA ** (-jnp.arange(half, dtype=F32) / half)
    invf = jnp.concatenate([inv_freq, inv_freq])[None, :]
    return pl.pallas_call(
        _rope_table_kernel,
        grid=(tokens // tm,),
        in_specs=[pl.BlockSpec((tm, 1), lambda i: (i, 0)),
                  pl.BlockSpec((1, HEAD_DIM), lambda i: (0, 0))],
        out_specs=[pl.BlockSpec((tm, HEAD_DIM), lambda i: (i, 0))] * 2,
        out_shape=[jax.ShapeDtypeStruct((tokens, HEAD_DIM), F32)] * 2,
        name="rope_tables",
    )(positions.reshape(tokens, 1), invf)


def _inproj_kernel(x_ref, nw_ref, wd_ref, wa_ref, wg_ref, tri_ref, alog_ref, dtb_ref, qnw_ref,
                   knw_ref, cos_ref, sin_ref, out_ref, gates_ref, gates_t_ref, hn_ref, *, n_heads,
                   n_delta_tiles):
    j = pl.program_id(1)

    @pl.when(j == 0)
    def _():
        x = x_ref[...]
        y = x * lax.rsqrt(jnp.mean(x * x, axis=-1, keepdims=True) + EPS) * nw_ref[...]
        hn = y.astype(BF16)
        hn_ref[...] = hn
        gl = jnp.dot(hn, wg_ref[...], preferred_element_type=F32)
        lane = lax.broadcasted_iota(jnp.int32, gl.shape, 1)
        beta = jax.nn.sigmoid(gl)
        t = gl + dtb_ref[...]
        softplus = jnp.maximum(t, 0.0) + jnp.log1p(jnp.exp(-jnp.abs(t)))
        g = -jnp.exp(alog_ref[...]) * softplus
        parts = jnp.dot(tri_ref[...], _split3(g), preferred_element_type=F32)
        gc = parts[:, :LANES] + parts[:, LANES:2 * LANES] + parts[:, 2 * LANES:]
        gates = jnp.where(lane < n_heads, beta, gc)
        gates_ref[...] = gates
        gates_t_ref[...] = gates.T[:2 * n_heads, :]

    @pl.when(j < n_delta_tiles)
    def _():
        out_ref[...] = jnp.dot(hn_ref[...], wd_ref[...],
                               preferred_element_type=F32).astype(out_ref.dtype)

    @pl.when(j == n_delta_tiles + 2)
    def _():
        out_ref[...] = jnp.dot(hn_ref[...], wa_ref[...],
                               preferred_element_type=F32).astype(out_ref.dtype)

    @pl.when(jnp.logical_or(j == n_delta_tiles, j == n_delta_tiles + 1))
    def _():
        acc = jnp.dot(hn_ref[...], wa_ref[...], preferred_element_type=F32)
        nw = jnp.where(j == n_delta_tiles, qnw_ref[...], knw_ref[...])
        out_scale = jnp.where(j == n_delta_tiles, HEAD_DIM ** -0.5, 1.0)
        cos = cos_ref[...]
        sin = sin_ref[...]
        pi = lax.broadcasted_iota(jnp.int32, (2 * HEAD_DIM, HEAD_DIM), 0)
        pj = lax.broadcasted_iota(jnp.int32, (2 * HEAD_DIM, HEAD_DIM), 1)
        ones2 = jnp.ones((2 * HEAD_DIM, HEAD_DIM), BF16)
        rot2 = ((pi % HEAD_DIM) == ((pj + HEAD_DIM // 2) % HEAD_DIM)).astype(BF16)
        for h in range(n_heads):
            xh = acc[:, h * HEAD_DIM:(h + 1) * HEAD_DIM]
            mean_sq = jnp.dot(_split2(xh * xh), ones2, preferred_element_type=F32) * (1.0 / HEAD_DIM)
            yh = xh * lax.rsqrt(mean_sq + EPS) * nw
            yh_rot = jnp.dot(_split2(yh), rot2, preferred_element_type=F32)
            roped = (yh * cos + yh_rot * sin) * out_scale
            out_ref[:, h * HEAD_DIM:(h + 1) * HEAD_DIM] = roped.astype(out_ref.dtype)


def _inproj(x2, nw, w_all, w_attn, alog, dtb, qnw, knw, cos_t, sin_t, n_heads):
    tokens, d_model = x2.shape
    width = n_heads * HEAD_DIM
    n_delta_tiles = 4
    n_tiles = n_delta_tiles + w_attn.shape[1] // width
    tm = 1024
    ri = lax.broadcasted_iota(jnp.int32, (tm, tm), 0)
    ci = lax.broadcasted_iota(jnp.int32, (tm, tm), 1)
    tri = jnp.logical_and(ri >= ci, (ri // DELTA_CHUNK) == (ci // DELTA_CHUNK)).astype(BF16)
    kern = functools.partial(_inproj_kernel, n_heads=n_heads, n_delta_tiles=n_delta_tiles)
    row = lambda i, j: (i, 0)
    const = lambda i, j: (0, 0)
    once = dict(pipeline_mode=pl.Buffered(1))
    return pl.pallas_call(
        kern,
        grid=(tokens // tm, n_tiles),
        in_specs=[pl.BlockSpec((tm, d_model), row),
                  pl.BlockSpec((1, d_model), const),
                  pl.BlockSpec((d_model, width), lambda i, j: (0, jnp.minimum(j, n_delta_tiles - 1))),
                  pl.BlockSpec((d_model, width), lambda i, j: (0, jnp.maximum(j - n_delta_tiles, 0))),
                  pl.BlockSpec((d_model, LANES), lambda i, j: (0, n_delta_tiles * width // LANES), **once),
                  pl.BlockSpec((tm, tm), const, **once),
                  pl.BlockSpec((1, LANES), const),
                  pl.BlockSpec((1, LANES), const),
                  pl.BlockSpec((1, HEAD_DIM), const),
                  pl.BlockSpec((1, HEAD_DIM), const),
                  pl.BlockSpec((tm, HEAD_DIM), row),
                  pl.BlockSpec((tm, HEAD_DIM), row)],
        out_specs=[pl.BlockSpec((tm, width), lambda i, j: (i, j)),
                   pl.BlockSpec((tm, LANES), row),
                   pl.BlockSpec((2 * n_heads, tm), lambda i, j: (0, i))],
        out_shape=[jax.ShapeDtypeStruct((tokens, n_tiles * width), BF16),
                   jax.ShapeDtypeStruct((tokens, LANES), F32),
                   jax.ShapeDtypeStruct((2 * n_heads, tokens), F32)],
        scratch_shapes=[pltpu.VMEM((tm, d_model), BF16)],
        compiler_params=pltpu.CompilerParams(
            dimension_semantics=("parallel", "arbitrary"), vmem_limit_bytes=VMEM_LIMIT),
        name="inproj",
    )(x2, nw, w_all, w_attn, w_all, tri, alog, dtb, qnw, knw, cos_t, sin_t)


def _delta_kernel(q_ref, k_ref, v_ref, z_ref, gates_ref, gates_t_ref, cwq_ref, cwk_ref, cwv_ref,
                  onw_ref, o_ref, state_ref, carry_ref, xpad_ref, *, n_heads, heads_per_step):
    hg = pl.program_id(1)
    s_idx = pl.program_id(2)
    ts = q_ref.shape[0]
    c = DELTA_CHUNK
    n_chunks = ts // c
    hd = HEAD_DIM

    @pl.when(s_idx == 0)
    def _():
        state_ref[...] = jnp.zeros_like(state_ref)
        carry_ref[...] = jnp.zeros_like(carry_ref)

    def conv_silu(x_ref, cw_ref, slot):
        w = cw_ref[...]
        halves = []
        for hh in range(heads_per_step):
            cols = slice(hh * hd, (hh + 1) * hd)
            xpad_ref[hh, pl.ds(0, SUBLANES, stride=2), :] = carry_ref[slot, :, cols]
            xpad_ref[hh, pl.ds(2 * SUBLANES, ts, stride=2), :] = x_ref[:, cols].astype(F32)
            y = jnp.zeros((ts, hd), F32)
            for j in range(CONV_WIDTH):
                off = 2 * (SUBLANES - (CONV_WIDTH - 1) + j)
                y = y + xpad_ref[hh, pl.ds(off, ts, stride=2), :] * w[j:j + 1, cols]
            halves.append(_silu(y))
            carry_ref[slot, :, cols] = xpad_ref[hh, pl.ds(2 * ts, SUBLANES, stride=2), :]
        return halves

    def l2n(t):
        return t * lax.rsqrt(jnp.sum(t * t, axis=-1, keepdims=True) + EPS)

    q_conv = conv_silu(q_ref, cwq_ref, 0)
    k_conv = conv_silu(k_ref, cwk_ref, 1)
    v_conv = conv_silu(v_ref, cwv_ref, 2)
    gt = gates_ref[...]
    lane = lax.broadcasted_iota(jnp.int32, gt.shape, 1)
    head_iota = lax.broadcasted_iota(jnp.int32, (n_heads, c), 0)

    ri = lax.broadcasted_iota(jnp.int32, (c, c), 0)
    ci = lax.broadcasted_iota(jnp.int32, (c, c), 1)
    incl = ri >= ci
    strict = ri > ci
    eye = (ri == ci).astype(F32)
    onw = onw_ref[...]

    qs, ks, vs, betas, gcbs, gcrs = [], [], [], [], [], []
    for hh in range(heads_per_step):
        head = hg * heads_per_step + hh
        q_h = l2n(q_conv[hh]) * (hd ** -0.5)
        k_h = l2n(k_conv[hh])
        v_h = v_conv[hh]
        beta_h = jnp.broadcast_to(
            jnp.sum(jnp.where(lane == head, gt, 0.0), axis=-1, keepdims=True), (ts, c))
        gc_h = jnp.broadcast_to(
            jnp.sum(jnp.where(lane == head + n_heads, gt, 0.0), axis=-1, keepdims=True), (ts, c))
        for ch in range(n_chunks):
            r = slice(ch * c, (ch + 1) * c)
            qs.append(q_h[r])
            ks.append(k_h[r])
            vs.append(v_h[r])
            betas.append(beta_h[r])
            gcbs.append(gc_h[r])
            gc_rows = gates_t_ref[n_heads:2 * n_heads, r]
            gcrs.append(jnp.sum(jnp.where(head_iota == head, gc_rows, 0.0),
                                axis=0, keepdims=True))
    units = range(heads_per_step * n_chunks)

    kks = [_mm_nt(ks[n], ks[n]) for n in units]
    qks = [_mm_nt(qs[n], ks[n]) for n in units]
    decays = [jnp.where(incl, jnp.exp(jnp.where(incl, gcbs[n] - gcrs[n], 0.0)), 0.0) for n in units]
    lowers = [jnp.where(strict, betas[n] * kks[n] * decays[n], 0.0) for n in units]
    same2 = (ri // 2) == (ci // 2)
    invs = [eye - jnp.where(same2, lowers[n], 0.0) for n in units]
    lowers_bf = [lowers[n].astype(BF16) for n in units]
    b = 2
    while b < c:
        off = jnp.logical_and((ri // (2 * b)) == (ci // (2 * b)), (ri // b) != (ci // b))
        off_bf = off.astype(F32).astype(BF16)
        tmp = [_mm(lowers_bf[n] * off_bf, invs[n]) for n in units]
        invs = [invs[n] - _mm(invs[n], tmp[n]) for n in units]
        b *= 2
    egcs = [jnp.exp(gcbs[n]) for n in units]
    us = [_mm(invs[n], vs[n] * betas[n]) for n in units]
    ws = [_mm(invs[n], ks[n] * (betas[n] * egcs[n])) for n in units]
    qks = [qks[n] * decays[n] for n in units]
    g_lasts = [gcbs[n][c - 1:c, :] for n in units]
    kd_ts = [(ks[n] * jnp.exp(g_lasts[n] - gcbs[n])).T for n in units]
    s_mix = [_mm(kd_ts[n], ws[n]) for n in units]
    s_add = [_mm(kd_ts[n], us[n]) for n in units]
    q_eff = [qs[n] * egcs[n] - _mm(qks[n], ws[n]) for n in units]
    o_loc = [_mm(qks[n], us[n]) for n in units]

    states = [state_ref[hh] for hh in range(heads_per_step)]
    outs = {}
    for ch in range(n_chunks):
        for hh in range(heads_per_step):
            n = hh * n_chunks + ch
            outs[n] = _mm(q_eff[n], states[hh]) + o_loc[n]
            states[hh] = (states[hh] * jnp.exp(g_lasts[n]) - _mm(s_mix[n], states[hh])) + s_add[n]
    for hh in range(heads_per_step):
        state_ref[hh] = states[hh]

    for hh in range(heads_per_step):
        cols = slice(hh * hd, (hh + 1) * hd)
        for ch in range(n_chunks):
            r = slice(ch * c, (ch + 1) * c)
            o_c = outs[hh * n_chunks + ch]
            o_n = o_c * lax.rsqrt(jnp.mean(o_c * o_c, axis=-1, keepdims=True) + EPS) * onw
            o_ref[r, cols] = (o_n * _silu(z_ref[r, cols].astype(F32))).astype(o_ref.dtype)


def _delta(proj, gates, gates_t, conv_w, onw, bsz, seq, n_heads):
    ts = 1024
    hb = 2
    s_tiles = seq // ts
    groups = n_heads // hb
    kern = functools.partial(_delta_kernel, n_heads=n_heads, heads_per_step=hb)

    def col(seg):
        return lambda b, g, s: (b * s_tiles + s, seg * groups + g)

    def cw(seg):
        return lambda b, g, s: (0, seg * groups + g)

    tile = (ts, hb * HEAD_DIM)
    return pl.pallas_call(
        kern,
        grid=(bsz, groups, s_tiles),
        in_specs=[pl.BlockSpec(tile, col(0)), pl.BlockSpec(tile, col(1)),
                  pl.BlockSpec(tile, col(2)), pl.BlockSpec(tile, col(3)),
                  pl.BlockSpec((ts, LANES), lambda b, g, s: (b * s_tiles + s, 0)),
                  pl.BlockSpec((2 * n_heads, ts), lambda b, g, s: (0, b * s_tiles + s)),
                  pl.BlockSpec((CONV_WIDTH, hb * HEAD_DIM), cw(0)),
                  pl.BlockSpec((CONV_WIDTH, hb * HEAD_DIM), cw(1)),
                  pl.BlockSpec((CONV_WIDTH, hb * HEAD_DIM), cw(2)),
                  pl.BlockSpec((1, HEAD_DIM), lambda b, g, s: (0, 0))],
        out_specs=pl.BlockSpec(tile, lambda b, g, s: (b * s_tiles + s, g)),
        out_shape=jax.ShapeDtypeStruct((bsz * seq, n_heads * HEAD_DIM), BF16),
        scratch_shapes=[pltpu.VMEM((hb, HEAD_DIM, HEAD_DIM), F32),
                        pltpu.VMEM((3, SUBLANES, hb * HEAD_DIM), F32),
                        pltpu.VMEM((hb, 2 * (ts + SUBLANES), HEAD_DIM), F32)],
        compiler_params=pltpu.CompilerParams(
            dimension_semantics=("parallel", "parallel", "arbitrary"),
            vmem_limit_bytes=VMEM_LIMIT),
        name="delta_rule",
    )(proj, proj, proj, proj, gates, gates_t, conv_w, conv_w, conv_w, onw)


def _attn_kernel(q_ref, k_ref, v_ref, onw_ref, o_ref, qf_ref, kf_ref, vf_ref, acc_ref, m_ref, l_ref):
    seq = q_ref.shape[0]
    hd = HEAD_DIM
    onw = onw_ref[...]
    order = sorted(DILATED_PATTERNS, key=lambda p: -p[1])
    qf_ref[...] = q_ref[...].astype(F32)
    kf_ref[...] = k_ref[...].astype(F32)
    vf_ref[...] = v_ref[...].astype(F32)

    for pi, (window, dil) in enumerate(order):
        first, last = pi == 0, pi == len(order) - 1
        blk = window // dil
        nb = seq // (blk * dil)
        run_len = min(nb, ATTN_UNROLL)
        runs_per_iter = ATTN_UNROLL // run_len
        runs_per_residue = nb // run_len
        qi = lax.broadcasted_iota(jnp.int32, (blk, blk), 0)
        ki = lax.broadcasted_iota(jnp.int32, (blk, blk), 1)
        cur_mask = ki <= qi
        prev_mask = ki >= qi
        ones_cols = jnp.ones((blk, hd), BF16)

        def body(it, carry, first=first, last=last, dil=dil, blk=blk, run_len=run_len,
                 runs_per_iter=runs_per_iter, runs_per_residue=runs_per_residue,
                 cur_mask=cur_mask, prev_mask=prev_mask, ones_cols=ones_cols):
            def rows(start):
                if dil == 1:
                    return pl.ds(pl.multiple_of(start, blk), blk)
                return pl.ds(start, blk, stride=dil)

            def tile(f32_ref, bf16_ref, start):
                if dil == 1:
                    return bf16_ref[rows(start), :]
                return f32_ref[rows(start), :].astype(BF16)

            blocks = []
            for rn in range(runs_per_iter):
                run = it * runs_per_iter + rn
                if runs_per_residue == 1:
                    r, j0 = run, 0
                    prev_k = prev_v = prev_ok = None
                else:
                    r = run // runs_per_residue
                    j0 = (run % runs_per_residue) * run_len
                    pstart = jnp.maximum(j0 - 1, 0) * (blk * dil) + r
                    prev_k, prev_v = tile(kf_ref, k_ref, pstart), tile(vf_ref, v_ref, pstart)
                    prev_ok = j0 > 0
                for jj in range(run_len):
                    start = (j0 + jj) * (blk * dil) + r
                    cur = rows(start)
                    k_t, v_t = tile(kf_ref, k_ref, start), tile(vf_ref, v_ref, start)
                    old = None if first else (m_ref[cur, :], l_ref[cur, :], acc_ref[cur, :])
                    blocks.append((cur, tile(qf_ref, q_ref, start), k_t, v_t, prev_k, prev_v,
                                   prev_ok, old))
                    prev_k, prev_v, prev_ok = k_t, v_t, None

            scores = []
            for cur, q_t, k_t, v_t, prev_k, prev_v, prev_ok, old in blocks:
                if prev_k is None:
                    scores.append((jnp.where(cur_mask, _mm_nt(q_t, k_t), NEG_BIG), None))
                else:
                    s2 = _mm_nt(q_t, jnp.concatenate([k_t, prev_k], axis=0))
                    pm = prev_mask if prev_ok is None else jnp.logical_and(prev_mask, prev_ok)
                    scores.append((jnp.where(cur_mask, s2[:, :blk], NEG_BIG),
                                   jnp.where(pm, s2[:, blk:], NEG_BIG)))

            probs = []
            for (cur, q_t, k_t, v_t, prev_k, prev_v, prev_ok, old), (s_cur, s_prev) in zip(blocks, scores):
                s_max = s_cur if s_prev is None else jnp.maximum(s_cur, s_prev)
                m_new = jnp.max(s_max, axis=-1, keepdims=True)
                if old is not None:
                    m_new = jnp.maximum(old[0], m_new)
                p_cur = jnp.exp(s_cur - m_new).astype(BF16)
                p_prev = None if s_prev is None else jnp.exp(s_prev - m_new).astype(BF16)
                probs.append((m_new, p_cur, p_prev))

            results = []
            for (cur, q_t, k_t, v_t, prev_k, prev_v, prev_ok, old), (m_new, p_cur, p_prev) in zip(blocks, probs):
                v_ext = jnp.concatenate([v_t, ones_cols], axis=1)
                if p_prev is None:
                    pv = jnp.dot(p_cur, v_ext, preferred_element_type=F32)
                else:
                    pv = jnp.dot(jnp.concatenate([p_cur, p_prev], axis=1),
                                 jnp.concatenate(
                                     [v_ext, jnp.concatenate([prev_v, ones_cols], axis=1)], axis=0),
                                 preferred_element_type=F32)
                acc_new, l_new = pv[:, :hd], pv[:, hd:]
                if old is not None:
                    alpha = jnp.exp(old[0] - m_new)
                    l_new = alpha * old[1] + l_new
                    acc_new = alpha * old[2] + acc_new
                results.append((cur, m_new, l_new, acc_new))

            for cur, m_new, l_new, acc_new in results:
                if last:
                    o = acc_new / l_new
                    o = o * lax.rsqrt(jnp.mean(o * o, axis=-1, keepdims=True) + EPS) * onw
                    o_ref[cur, :] = o.astype(o_ref.dtype)
                else:
                    m_ref[cur, :] = jnp.broadcast_to(m_new, (blk, hd))
                    l_ref[cur, :] = l_new
                    acc_ref[cur, :] = acc_new
            return carry

        lax.fori_loop(0, dil * nb // ATTN_UNROLL, body, 0)


def _attention(proj, onw, bsz, seq, n_heads, col0):
    tile = (seq, HEAD_DIM)

    def col(seg):
        return lambda b, h: (b, col0 + seg * n_heads + h)

    return pl.pallas_call(
        _attn_kernel,
        grid=(bsz, n_heads),
        in_specs=[pl.BlockSpec(tile, col(0)), pl.BlockSpec(tile, col(1)),
                  pl.BlockSpec(tile, col(2)),
                  pl.BlockSpec((1, HEAD_DIM), lambda b, h: (0, 0))],
        out_specs=pl.BlockSpec(tile, lambda b, h: (b, h)),
        out_shape=jax.ShapeDtypeStruct((bsz * seq, n_heads * HEAD_DIM), BF16),
        scratch_shapes=[pltpu.VMEM(tile, F32)] * 6,
        compiler_params=pltpu.CompilerParams(
            dimension_semantics=("parallel", "parallel"), vmem_limit_bytes=VMEM_LIMIT),
        name="dilated_attention",
    )(proj, proj, proj, onw)


def _outproj_kernel(oa_ref, ob_ref, wa_ref, wb_ref, x_ref, fnw_ref, x1_ref, h2_ref):
    acc = (jnp.dot(oa_ref[...], wa_ref[...], preferred_element_type=F32)
           + jnp.dot(ob_ref[...], wb_ref[...], preferred_element_type=F32))
    x1 = x_ref[...] + acc
    x1_ref[...] = x1
    h2 = x1 * lax.rsqrt(jnp.mean(x1 * x1, axis=-1, keepdims=True) + EPS) * fnw_ref[...]
    h2_ref[...] = h2.astype(h2_ref.dtype)


def _outproj(o_a, o_b, w_out, x2, fnw):
    tokens, d_model = x2.shape
    wa_rows = o_a.shape[1]
    tm = 512
    row = lambda i: (i, 0)
    return pl.pallas_call(
        _outproj_kernel,
        grid=(tokens // tm,),
        in_specs=[pl.BlockSpec((tm, wa_rows), row),
                  pl.BlockSpec((tm, o_b.shape[1]), row),
                  pl.BlockSpec((wa_rows, d_model), lambda i: (0, 0)),
                  pl.BlockSpec((o_b.shape[1], d_model), lambda i: (1, 0)),
                  pl.BlockSpec((tm, d_model), row),
                  pl.BlockSpec((1, d_model), lambda i: (0, 0))],
        out_specs=[pl.BlockSpec((tm, d_model), row), pl.BlockSpec((tm, d_model), row)],
        out_shape=[jax.ShapeDtypeStruct((tokens, d_model), F32),
                   jax.ShapeDtypeStruct((tokens, d_model), BF16)],
        compiler_params=pltpu.CompilerParams(
            dimension_semantics=("parallel",), vmem_limit_bytes=VMEM_LIMIT),
        name="outproj",
    )(o_a, o_b, w_out, w_out, x2, fnw)


def _ffn_kernel(h_ref, wg_ref, wu_ref, wd_ref, x1_ref, o_ref):
    @pl.when(pl.program_id(1) == 0)
    def _():
        o_ref[...] = x1_ref[...]

    h = h_ref[...]
    g = jnp.dot(h, wg_ref[...], preferred_element_type=F32)
    u = jnp.dot(h, wu_ref[...], preferred_element_type=F32)
    a = (_silu(g) * u).astype(BF16)
    o_ref[...] += jnp.dot(a, wd_ref[...], preferred_element_type=F32)


def _ffn(h2, w_gate_up, w_down, x1):
    tokens, d_model = x1.shape
    d_ff = w_down.shape[0]
    tm, tf = 1024, 512
    nf = d_ff // tf
    row = lambda i, f: (i, 0)
    return pl.pallas_call(
        _ffn_kernel,
        grid=(tokens // tm, nf),
        in_specs=[pl.BlockSpec((tm, d_model), row),
                  pl.BlockSpec((d_model, tf), lambda i, f: (0, f)),
                  pl.BlockSpec((d_model, tf), lambda i, f: (0, nf + f)),
                  pl.BlockSpec((tf, d_model), lambda i, f: (f, 0)),
                  pl.BlockSpec((tm, d_model), row)],
        out_specs=pl.BlockSpec((tm, d_model), row),
        out_shape=jax.ShapeDtypeStruct((tokens, d_model), F32),
        compiler_params=pltpu.CompilerParams(
            dimension_semantics=("parallel", "arbitrary"), vmem_limit_bytes=VMEM_LIMIT),
        name="swiglu",
    )(h2, w_gate_up, w_gate_up, w_down, x1)


def _pad_lanes(v, offset):
    out = jnp.zeros((1, LANES), F32)
    return lax.dynamic_update_slice(out, v.astype(F32)[None, :], (0, offset))


def kernel(x, positions, attn_norm_w, w_in, conv_w, a_log, dt_bias, delta_out_norm_w, q_norm_w,
           k_norm_w, attn_out_norm_w, w_out, ffn_norm_w, w_gate_up, w_down):
    bsz, seq, d_model = x.shape
    depth = w_in.shape[0]
    n_heads = a_log.shape[1]
    width = n_heads * HEAD_DIM
    gate_lo, gate_hi = 4 * width, 4 * width + 2 * n_heads
    assert 2 * n_heads <= LANES and w_in.shape[2] == gate_hi + 3 * width

    cos_t, sin_t = _rope_tables(positions)
    x2 = x.reshape(bsz * seq, d_model)
    for l in range(depth):
        w_all = w_in[l].astype(BF16)
        proj, gates, gates_t = _inproj(
            x2, attn_norm_w[l][None, :], w_all, w_all[:, gate_hi:],
            _pad_lanes(a_log[l], n_heads), _pad_lanes(dt_bias[l], n_heads),
            q_norm_w[l][None, :], k_norm_w[l][None, :], cos_t, sin_t, n_heads)
        o_a = _delta(proj, gates, gates_t, conv_w[l], delta_out_norm_w[l][None, :], bsz, seq, n_heads)
        o_b = _attention(proj, attn_out_norm_w[l][None, :], bsz, seq, n_heads, 4 * n_heads)
        x1, h2 = _outproj(o_a, o_b, w_out[l].astype(BF16), x2, ffn_norm_w[l][None, :])
        x2 = _ffn(h2, w_gate_up[l].astype(BF16), w_down[l].astype(BF16), x1)
    return x2.reshape(bsz, seq, d_model)
```

```python
import functools

import jax
import jax.numpy as jnp
from jax import lax
from jax.experimental import pallas as pl
from jax.experimental.pallas import tpu as pltpu

HEAD_DIM = 128
CONV_WIDTH = 4
DILATED_PATTERNS = ((128, 1), (512, 4), (2048, 16))
ROPE_THETA = 10000.0
EPS = 1e-6

LANES = 128
SUBLANES = 8
DELTA_CHUNK = 128
NEG_BIG = -1e30
ATTN_UNROLL = 8

F32 = jnp.float32
BF16 = jnp.bfloat16

VMEM_LIMIT = 60 * 1024 * 1024


def _mm(a, b):
    return jnp.dot(a.astype(BF16), b.astype(BF16), preferred_element_type=F32)


def _mm_nt(a, b):
    return lax.dot_general(a.astype(BF16), b.astype(BF16), (((1,), (1,)), ((), ())),
                           preferred_element_type=F32)


def _split3(x):
    hi = x.astype(BF16)
    rest = x - hi.astype(F32)
    mid = rest.astype(BF16)
    lo = (rest - mid.astype(F32)).astype(BF16)
    return jnp.concatenate([hi, mid, lo], axis=1)


def _silu(x):
    return x * jax.nn.sigmoid(x)


def _rope_table_kernel(pos_ref, invf_ref, cos_ref, sin_ref):
    ang = pos_ref[...].astype(F32) * invf_ref[...]
    lane = lax.broadcasted_iota(jnp.int32, ang.shape, 1)
    cos_ref[...] = jnp.cos(ang)
    s = jnp.sin(ang)
    sin_ref[...] = jnp.where(lane < HEAD_DIM // 2, -s, s)


def _rope_tables(positions):
    tokens = positions.size
    tm = 1024
    half = HEAD_DIM // 2
    inv_freq = ROPE_THETA ** (-jnp.arange(half, dtype=F32) / half)
    invf = jnp.concatenate([inv_freq, inv_freq])[None, :]
    return pl.pallas_call(
        _rope_table_kernel,
        grid=(tokens // tm,),
        in_specs=[pl.BlockSpec((tm, 1), lambda i: (i, 0)),
                  pl.BlockSpec((1, HEAD_DIM), lambda i: (0, 0))],
        out_specs=[pl.BlockSpec((tm, HEAD_DIM), lambda i: (i, 0))] * 2,
        out_shape=[jax.ShapeDtypeStruct((tokens, HEAD_DIM), F32)] * 2,
        name="rope_tables",
    )(positions.reshape(tokens, 1), invf)


def _inproj_kernel(x_ref, nw_ref, wd_ref, wa_ref, wg_ref, alog_ref, dtb_ref, qnw_ref,
                   knw_ref, cos_ref, sin_ref, out_ref, gates_ref, gates_t_ref, hn_ref, *, n_heads,
                   n_delta_tiles):
    j = pl.program_id(1)

    @pl.when(j == 0)
    def _():
        x = x_ref[...]
        y = x * lax.rsqrt(jnp.mean(x * x, axis=-1, keepdims=True) + EPS) * nw_ref[...]
        hn = y.astype(BF16)
        hn_ref[...] = hn
        gl = jnp.dot(hn, wg_ref[...], preferred_element_type=F32)
        lane = lax.broadcasted_iota(jnp.int32, gl.shape, 1)
        beta = jax.nn.sigmoid(gl)
        t = gl + dtb_ref[...]
        softplus = jnp.maximum(t, 0.0) + jnp.log1p(jnp.exp(-jnp.abs(t)))
        g = -jnp.exp(alog_ref[...]) * softplus
        c = DELTA_CHUNK
        tri = (lax.broadcasted_iota(jnp.int32, (c, c), 0)
               >= lax.broadcasted_iota(jnp.int32, (c, c), 1)).astype(BF16)
        g3 = _split3(g)
        parts = jnp.concatenate(
            [jnp.dot(tri, g3[ch * c:(ch + 1) * c], preferred_element_type=F32)
             for ch in range(g.shape[0] // c)], axis=0)
        gc = parts[:, :LANES] + parts[:, LANES:2 * LANES] + parts[:, 2 * LANES:]
        gates = jnp.where(lane < n_heads, beta, gc)
        gates_ref[...] = gates
        gates_t_ref[...] = gates.T[:2 * n_heads, :]

    @pl.when(j < n_delta_tiles)
    def _():
        out_ref[...] = jnp.dot(hn_ref[...], wd_ref[...],
                               preferred_element_type=F32).astype(out_ref.dtype)

    @pl.when(j == n_delta_tiles + 2)
    def _():
        out_ref[...] = jnp.dot(hn_ref[...], wa_ref[...],
                               preferred_element_type=F32).astype(out_ref.dtype)

    @pl.when(jnp.logical_or(j == n_delta_tiles, j == n_delta_tiles + 1))
    def _():
        acc = jnp.dot(hn_ref[...], wa_ref[...], preferred_element_type=F32)
        nw = jnp.where(j == n_delta_tiles, qnw_ref[...], knw_ref[...])
        out_scale = jnp.where(j == n_delta_tiles, HEAD_DIM ** -0.5, 1.0)
        half = HEAD_DIM // 2
        nw_rot = jnp.concatenate([nw[:, half:], nw[:, :half]], axis=1)
        cos_w = cos_ref[...] * (nw * out_scale)
        sin_w = sin_ref[...] * (nw_rot * out_scale)
        pi = lax.broadcasted_iota(jnp.int32, (HEAD_DIM, HEAD_DIM), 0)
        pj = lax.broadcasted_iota(jnp.int32, (HEAD_DIM, HEAD_DIM), 1)
        ones_m = jnp.ones((HEAD_DIM, HEAD_DIM), BF16)
        rot_m = (pi == ((pj + half) % HEAD_DIM)).astype(BF16)
        for h in range(n_heads):
            xh = acc[:, h * HEAD_DIM:(h + 1) * HEAD_DIM]
            x_rot = jnp.dot(xh.astype(BF16), rot_m, preferred_element_type=F32)
            mean_sq = jnp.dot((xh * xh).astype(BF16), ones_m,
                              preferred_element_type=F32) * (1.0 / HEAD_DIM)
            roped = lax.rsqrt(mean_sq + EPS) * (xh * cos_w + x_rot * sin_w)
            out_ref[:, h * HEAD_DIM:(h + 1) * HEAD_DIM] = roped.astype(out_ref.dtype)


def _inproj(x2, nw, w_all, w_attn, alog, dtb, qnw, knw, cos_t, sin_t, n_heads):
    tokens, d_model = x2.shape
    width = n_heads * HEAD_DIM
    n_delta_tiles = 4
    n_tiles = n_delta_tiles + w_attn.shape[1] // width
    tm = 1024
    kern = functools.partial(_inproj_kernel, n_heads=n_heads, n_delta_tiles=n_delta_tiles)
    row = lambda i, j: (i, 0)
    const = lambda i, j: (0, 0)
    once = dict(pipeline_mode=pl.Buffered(1))
    return pl.pallas_call(
        kern,
        grid=(tokens // tm, n_tiles),
        in_specs=[pl.BlockSpec((tm, d_model), row),
                  pl.BlockSpec((1, d_model), const),
                  pl.BlockSpec((d_model, width), lambda i, j: (0, jnp.minimum(j, n_delta_tiles - 1))),
                  pl.BlockSpec((d_model, width), lambda i, j: (0, jnp.maximum(j - n_delta_tiles, 0))),
                  pl.BlockSpec((d_model, LANES), lambda i, j: (0, n_delta_tiles * width // LANES), **once),
                  pl.BlockSpec((1, LANES), const),
                  pl.BlockSpec((1, LANES), const),
                  pl.BlockSpec((1, HEAD_DIM), const),
                  pl.BlockSpec((1, HEAD_DIM), const),
                  pl.BlockSpec((tm, HEAD_DIM), row),
                  pl.BlockSpec((tm, HEAD_DIM), row)],
        out_specs=[pl.BlockSpec((tm, width), lambda i, j: (i, j)),
                   pl.BlockSpec((tm, LANES), row),
                   pl.BlockSpec((2 * n_heads, tm), lambda i, j: (0, i))],
        out_shape=[jax.ShapeDtypeStruct((tokens, n_tiles * width), BF16),
                   jax.ShapeDtypeStruct((tokens, LANES), F32),
                   jax.ShapeDtypeStruct((2 * n_heads, tokens), F32)],
        scratch_shapes=[pltpu.VMEM((tm, d_model), BF16)],
        compiler_params=pltpu.CompilerParams(
            dimension_semantics=("parallel", "arbitrary"), vmem_limit_bytes=VMEM_LIMIT),
        name="inproj",
    )(x2, nw, w_all, w_attn, w_all, alog, dtb, qnw, knw, cos_t, sin_t)


def _delta_kernel(q_ref, k_ref, v_ref, z_ref, gates_ref, gates_t_ref, cwq_ref, cwk_ref, cwv_ref,
                  onw_ref, o_ref, state_ref, carry_ref, xpad_ref, *, n_heads, heads_per_step):
    hg = pl.program_id(1)
    s_idx = pl.program_id(2)
    ts = q_ref.shape[0]
    c = DELTA_CHUNK
    n_chunks = ts // c
    hd = HEAD_DIM

    @pl.when(s_idx == 0)
    def _():
        state_ref[...] = jnp.zeros_like(state_ref)
        carry_ref[...] = jnp.zeros_like(carry_ref)

    def conv_silu(x_ref, cw_ref, slot):
        w = cw_ref[...]
        halves = []
        for hh in range(heads_per_step):
            cols = slice(hh * hd, (hh + 1) * hd)
            xpad_ref[hh, pl.ds(0, SUBLANES, stride=2), :] = carry_ref[slot, :, cols]
            xpad_ref[hh, pl.ds(2 * SUBLANES, ts, stride=2), :] = x_ref[:, cols].astype(F32)
            y = jnp.zeros((ts, hd), F32)
            for j in range(CONV_WIDTH):
                off = 2 * (SUBLANES - (CONV_WIDTH - 1) + j)
                y = y + xpad_ref[hh, pl.ds(off, ts, stride=2), :] * w[j:j + 1, cols]
            halves.append(_silu(y))
            carry_ref[slot, :, cols] = xpad_ref[hh, pl.ds(2 * ts, SUBLANES, stride=2), :]
        return halves

    def l2n(t):
        return t * lax.rsqrt(jnp.sum(t * t, axis=-1, keepdims=True) + EPS)

    q_conv = conv_silu(q_ref, cwq_ref, 0)
    k_conv = conv_silu(k_ref, cwk_ref, 1)
    v_conv = conv_silu(v_ref, cwv_ref, 2)
    gt = gates_ref[...]
    lane = lax.broadcasted_iota(jnp.int32, gt.shape, 1)
    head_iota = lax.broadcasted_iota(jnp.int32, (n_heads, c), 0)

    ri = lax.broadcasted_iota(jnp.int32, (c, c), 0)
    ci = lax.broadcasted_iota(jnp.int32, (c, c), 1)
    incl = ri >= ci
    strict = ri > ci
    eye = (ri == ci).astype(F32)
    onw = onw_ref[...]

    qs, ks, vs, betas, gcbs, gcrs = [], [], [], [], [], []
    for hh in range(heads_per_step):
        head = hg * heads_per_step + hh
        q_h = l2n(q_conv[hh]) * (hd ** -0.5)
        k_h = l2n(k_conv[hh])
        v_h = v_conv[hh]
        beta_h = jnp.broadcast_to(
            jnp.sum(jnp.where(lane == head, gt, 0.0), axis=-1, keepdims=True), (ts, c))
        gc_h = jnp.broadcast_to(
            jnp.sum(jnp.where(lane == head + n_heads, gt, 0.0), axis=-1, keepdims=True), (ts, c))
        for ch in range(n_chunks):
            r = slice(ch * c, (ch + 1) * c)
            qs.append(q_h[r])
            ks.append(k_h[r])
            vs.append(v_h[r])
            betas.append(beta_h[r])
            gcbs.append(gc_h[r])
            gc_rows = gates_t_ref[n_heads:2 * n_heads, r]
            gcrs.append(jnp.sum(jnp.where(head_iota == head, gc_rows, 0.0),
                                axis=0, keepdims=True))
    units = range(heads_per_step * n_chunks)

    kks = [_mm_nt(ks[n], ks[n]) for n in units]
    qks = [_mm_nt(qs[n], ks[n]) for n in units]
    decays = [jnp.where(incl, jnp.exp(jnp.where(incl, gcbs[n] - gcrs[n], 0.0)), 0.0) for n in units]
    lowers = [jnp.where(strict, betas[n] * kks[n] * decays[n], 0.0) for n in units]
    same2 = (ri // 2) == (ci // 2)
    invs = [eye - jnp.where(same2, lowers[n], 0.0) for n in units]
    lowers_bf = [lowers[n].astype(BF16) for n in units]
    b = 2
    while b < c:
        off = jnp.logical_and((ri // (2 * b)) == (ci // (2 * b)), (ri // b) != (ci // b))
        off_bf = off.astype(F32).astype(BF16)
        tmp = [_mm(lowers_bf[n] * off_bf, invs[n]) for n in units]
        invs = [invs[n] - _mm(invs[n], tmp[n]) for n in units]
        b *= 2
    egcs = [jnp.exp(gcbs[n]) for n in units]
    us = [_mm(invs[n], vs[n] * betas[n]) for n in units]
    ws = [_mm(invs[n], ks[n] * (betas[n] * egcs[n])) for n in units]
    qks = [qks[n] * decays[n] for n in units]
    g_lasts = [gcbs[n][c - 1:c, :] for n in units]
    kd_ts = [(ks[n] * jnp.exp(g_lasts[n] - gcbs[n])).T for n in units]
    s_mix = [_mm(kd_ts[n], ws[n]) for n in units]
    s_add = [_mm(kd_ts[n], us[n]) for n in units]
    q_eff = [qs[n] * egcs[n] - _mm(qks[n], ws[n]) for n in units]
    o_loc = [_mm(qks[n], us[n]) for n in units]

    states = [state_ref[hh] for hh in range(heads_per_step)]
    outs = {}
    for ch in range(n_chunks):
        for hh in range(heads_per_step):
            n = hh * n_chunks + ch
            outs[n] = _mm(q_eff[n], states[hh]) + o_loc[n]
            states[hh] = (states[hh] * jnp.exp(g_lasts[n]) - _mm(s_mix[n], states[hh])) + s_add[n]
    for hh in range(heads_per_step):
        state_ref[hh] = states[hh]

    for hh in range(heads_per_step):
        cols = slice(hh * hd, (hh + 1) * hd)
        for ch in range(n_chunks):
            r = slice(ch * c, (ch + 1) * c)
            o_c = outs[hh * n_chunks + ch]
            o_n = o_c * lax.rsqrt(jnp.mean(o_c * o_c, axis=-1, keepdims=True) + EPS) * onw
            o_ref[r, cols] = (o_n * _silu(z_ref[r, cols].astype(F32))).astype(o_ref.dtype)


def _delta(proj, gates, gates_t, conv_w, onw, bsz, seq, n_heads):
    ts = 1024
    hb = 2
    s_tiles = seq // ts
    groups = n_heads // hb
    kern = functools.partial(_delta_kernel, n_heads=n_heads, heads_per_step=hb)

    def col(seg):
        return lambda b, g, s: (b * s_tiles + s, seg * groups + g)

    def cw(seg):
        return lambda b, g, s: (0, seg * groups + g)

    tile = (ts, hb * HEAD_DIM)
    return pl.pallas_call(
        kern,
        grid=(bsz, groups, s_tiles),
        in_specs=[pl.BlockSpec(tile, col(0)), pl.BlockSpec(tile, col(1)),
                  pl.BlockSpec(tile, col(2)), pl.BlockSpec(tile, col(3)),
                  pl.BlockSpec((ts, LANES), lambda b, g, s: (b * s_tiles + s, 0)),
                  pl.BlockSpec((2 * n_heads, ts), lambda b, g, s: (0, b * s_tiles + s)),
                  pl.BlockSpec((CONV_WIDTH, hb * HEAD_DIM), cw(0)),
                  pl.BlockSpec((CONV_WIDTH, hb * HEAD_DIM), cw(1)),
                  pl.BlockSpec((CONV_WIDTH, hb * HEAD_DIM), cw(2)),
                  pl.BlockSpec((1, HEAD_DIM), lambda b, g, s: (0, 0))],
        out_specs=pl.BlockSpec(tile, lambda b, g, s: (b * s_tiles + s, g)),
        out_shape=jax.ShapeDtypeStruct((bsz * seq, n_heads * HEAD_DIM), BF16),
        scratch_shapes=[pltpu.VMEM((hb, HEAD_DIM, HEAD_DIM), F32),
                        pltpu.VMEM((3, SUBLANES, hb * HEAD_DIM), F32),
                        pltpu.VMEM((hb, 2 * (ts + SUBLANES), HEAD_DIM), F32)],
        compiler_params=pltpu.CompilerParams(
            dimension_semantics=("parallel", "parallel", "arbitrary"),
            vmem_limit_bytes=VMEM_LIMIT),
        name="delta_rule",
    )(proj, proj, proj, proj, gates, gates_t, conv_w, conv_w, conv_w, onw)


def _attn_kernel(q_ref, k_ref, v_ref, onw_ref, o_ref, qf_ref, kf_ref, vf_ref, acc_ref, m_ref, l_ref):
    seq = q_ref.shape[0]
    hd = HEAD_DIM
    onw = onw_ref[...]
    order = sorted(DILATED_PATTERNS, key=lambda p: -p[1])
    qf_ref[...] = q_ref[...].astype(F32)
    kf_ref[...] = k_ref[...].astype(F32)
    vf_ref[...] = v_ref[...].astype(F32)

    for pi, (window, dil) in enumerate(order):
        first, last = pi == 0, pi == len(order) - 1
        blk = window // dil
        nb = seq // (blk * dil)
        run_len = min(nb, ATTN_UNROLL)
        runs_per_iter = ATTN_UNROLL // run_len
        runs_per_residue = nb // run_len
        qi = lax.broadcasted_iota(jnp.int32, (blk, blk), 0)
        ki = lax.broadcasted_iota(jnp.int32, (blk, blk), 1)
        cur_mask = ki <= qi
        prev_mask = ki >= qi
        ones_cols = jnp.ones((blk, hd), BF16)

        def body(it, carry, first=first, last=last, dil=dil, blk=blk, run_len=run_len,
                 runs_per_iter=runs_per_iter, runs_per_residue=runs_per_residue,
                 cur_mask=cur_mask, prev_mask=prev_mask, ones_cols=ones_cols):
            def rows(start):
                if dil == 1:
                    return pl.ds(pl.multiple_of(start, blk), blk)
                return pl.ds(start, blk, stride=dil)

            def tile(f32_ref, bf16_ref, start):
                if dil == 1:
                    return bf16_ref[rows(start), :]
                return f32_ref[rows(start), :].astype(BF16)

            blocks = []
            for rn in range(runs_per_iter):
                run = it * runs_per_iter + rn
                if runs_per_residue == 1:
                    r, j0 = run, 0
                    prev_k = prev_v = prev_ok = None
                else:
                    r = run // runs_per_residue
                    j0 = (run % runs_per_residue) * run_len
                    pstart = jnp.maximum(j0 - 1, 0) * (blk * dil) + r
                    prev_k, prev_v = tile(kf_ref, k_ref, pstart), tile(vf_ref, v_ref, pstart)
                    prev_ok = j0 > 0
                for jj in range(run_len):
                    start = (j0 + jj) * (blk * dil) + r
                    cur = rows(start)
                    k_t, v_t = tile(kf_ref, k_ref, start), tile(vf_ref, v_ref, start)
                    old = None if first else (m_ref[cur, :], l_ref[cur, :], acc_ref[cur, :])
                    blocks.append((cur, tile(qf_ref, q_ref, start), k_t, v_t, prev_k, prev_v,
                                   prev_ok, old))
                    prev_k, prev_v, prev_ok = k_t, v_t, None

            scores = []
            for cur, q_t, k_t, v_t, prev_k, prev_v, prev_ok, old in blocks:
                if prev_k is None:
                    scores.append((jnp.where(cur_mask, _mm_nt(q_t, k_t), NEG_BIG), None))
                else:
                    s2 = _mm_nt(q_t, jnp.concatenate([k_t, prev_k], axis=0))
                    pm = prev_mask if prev_ok is None else jnp.logical_and(prev_mask, prev_ok)
                    scores.append((jnp.where(cur_mask, s2[:, :blk], NEG_BIG),
                                   jnp.where(pm, s2[:, blk:], NEG_BIG)))

            probs = []
            for (cur, q_t, k_t, v_t, prev_k, prev_v, prev_ok, old), (s_cur, s_prev) in zip(blocks, scores):
                s_max = s_cur if s_prev is None else jnp.maximum(s_cur, s_prev)
                m_new = jnp.max(s_max, axis=-1, keepdims=True)
                if old is not None:
                    m_new = jnp.maximum(old[0], m_new)
                p_cur = jnp.exp(s_cur - m_new).astype(BF16)
                p_prev = None if s_prev is None else jnp.exp(s_prev - m_new).astype(BF16)
                probs.append((m_new, p_cur, p_prev))

            results = []
            for (cur, q_t, k_t, v_t, prev_k, prev_v, prev_ok, old), (m_new, p_cur, p_prev) in zip(blocks, probs):
                v_ext = jnp.concatenate([v_t, ones_cols], axis=1)
                if p_prev is None:
                    pv = jnp.dot(p_cur, v_ext, preferred_element_type=F32)
                else:
                    pv = jnp.dot(jnp.concatenate([p_cur, p_prev], axis=1),
                                 jnp.concatenate(
                                     [v_ext, jnp.concatenate([prev_v, ones_cols], axis=1)], axis=0),
                                 preferred_element_type=F32)
                acc_new, l_new = pv[:, :hd], pv[:, hd:]
                if old is not None:
                    alpha = jnp.exp(old[0] - m_new)
                    l_new = alpha * old[1] + l_new
                    acc_new = alpha * old[2] + acc_new
                results.append((cur, m_new, l_new, acc_new))

            for cur, m_new, l_new, acc_new in results:
                if last:
                    o = acc_new / l_new
                    o = o * lax.rsqrt(jnp.mean(o * o, axis=-1, keepdims=True) + EPS) * onw
                    o_ref[cur, :] = o.astype(o_ref.dtype)
                else:
                    m_ref[cur, :] = jnp.broadcast_to(m_new, (blk, hd))
                    l_ref[cur, :] = l_new
                    acc_ref[cur, :] = acc_new
            return carry

        lax.fori_loop(0, dil * nb // ATTN_UNROLL, body, 0)


def _attention(proj, onw, bsz, seq, n_heads, col0):
    tile = (seq, HEAD_DIM)

    def col(seg):
        return lambda b, h: (b, col0 + seg * n_heads + h)

    return pl.pallas_call(
        _attn_kernel,
        grid=(bsz, n_heads),
        in_specs=[pl.BlockSpec(tile, col(0)), pl.BlockSpec(tile, col(1)),
                  pl.BlockSpec(tile, col(2)),
                  pl.BlockSpec((1, HEAD_DIM), lambda b, h: (0, 0))],
        out_specs=pl.BlockSpec(tile, lambda b, h: (b, h)),
        out_shape=jax.ShapeDtypeStruct((bsz * seq, n_heads * HEAD_DIM), BF16),
        scratch_shapes=[pltpu.VMEM(tile, F32)] * 6,
        compiler_params=pltpu.CompilerParams(
            dimension_semantics=("parallel", "parallel"), vmem_limit_bytes=VMEM_LIMIT),
        name="dilated_attention",
    )(proj, proj, proj, onw)


def _outproj_kernel(oa_ref, ob_ref, wa_ref, wb_ref, x_ref, fnw_ref, x1_ref, h2_ref):
    acc = (jnp.dot(oa_ref[...], wa_ref[...], preferred_element_type=F32)
           + jnp.dot(ob_ref[...], wb_ref[...], preferred_element_type=F32))
    x1 = x_ref[...] + acc
    x1_ref[...] = x1
    h2 = x1 * lax.rsqrt(jnp.mean(x1 * x1, axis=-1, keepdims=True) + EPS) * fnw_ref[...]
    h2_ref[...] = h2.astype(h2_ref.dtype)


def _outproj(o_a, o_b, w_out, x2, fnw):
    tokens, d_model = x2.shape
    wa_rows = o_a.shape[1]
    tm = 512
    row = lambda i: (i, 0)
    return pl.pallas_call(
        _outproj_kernel,
        grid=(tokens // tm,),
        in_specs=[pl.BlockSpec((tm, wa_rows), row),
                  pl.BlockSpec((tm, o_b.shape[1]), row),
                  pl.BlockSpec((wa_rows, d_model), lambda i: (0, 0)),
                  pl.BlockSpec((o_b.shape[1], d_model), lambda i: (1, 0)),
                  pl.BlockSpec((tm, d_model), row),
                  pl.BlockSpec((1, d_model), lambda i: (0, 0))],
        out_specs=[pl.BlockSpec((tm, d_model), row), pl.BlockSpec((tm, d_model), row)],
        out_shape=[jax.ShapeDtypeStruct((tokens, d_model), F32),
                   jax.ShapeDtypeStruct((tokens, d_model), BF16)],
        compiler_params=pltpu.CompilerParams(
            dimension_semantics=("parallel",), vmem_limit_bytes=VMEM_LIMIT),
        name="outproj",
    )(o_a, o_b, w_out, w_out, x2, fnw)


def _ffn_kernel(h_ref, wg_ref, wu_ref, wd_ref, x1_ref, o_ref):
    @pl.when(pl.program_id(1) == 0)
    def _():
        o_ref[...] = x1_ref[...]

    h = h_ref[...]
    g = jnp.dot(h, wg_ref[...], preferred_element_type=F32)
    u = jnp.dot(h, wu_ref[...], preferred_element_type=F32)
    a = (_silu(g) * u).astype(BF16)
    o_ref[...] += jnp.dot(a, wd_ref[...], preferred_element_type=F32)


def _ffn(h2, w_gate_up, w_down, x1):
    tokens, d_model = x1.shape
    d_ff = w_down.shape[0]
    tm, tf = 1024, 512
    nf = d_ff // tf
    row = lambda i, f: (i, 0)
    return pl.pallas_call(
        _ffn_kernel,
        grid=(tokens // tm, nf),
        in_specs=[pl.BlockSpec((tm, d_model), row),
                  pl.BlockSpec((d_model, tf), lambda i, f: (0, f)),
                  pl.BlockSpec((d_model, tf), lambda i, f: (0, nf + f)),
                  pl.BlockSpec((tf, d_model), lambda i, f: (f, 0)),
                  pl.BlockSpec((tm, d_model), row)],
        out_specs=pl.BlockSpec((tm, d_model), row),
        out_shape=jax.ShapeDtypeStruct((tokens, d_model), F32),
        compiler_params=pltpu.CompilerParams(
            dimension_semantics=("parallel", "arbitrary"), vmem_limit_bytes=VMEM_LIMIT),
        name="swiglu",
    )(h2, w_gate_up, w_gate_up, w_down, x1)


def _pad_lanes(v, offset):
    out = jnp.zeros((1, LANES), F32)
    return lax.dynamic_update_slice(out, v.astype(F32)[None, :], (0, offset))


def kernel(x, positions, attn_norm_w, w_in, conv_w, a_log, dt_bias, delta_out_norm_w, q_norm_w,
           k_norm_w, attn_out_norm_w, w_out, ffn_norm_w, w_gate_up, w_down):
    bsz, seq, d_model = x.shape
    depth = w_in.shape[0]
    n_heads = a_log.shape[1]
    width = n_heads * HEAD_DIM
    gate_lo, gate_hi = 4 * width, 4 * width + 2 * n_heads
    assert 2 * n_heads <= LANES and w_in.shape[2] == gate_hi + 3 * width

    cos_t, sin_t = _rope_tables(positions)
    x2 = x.reshape(bsz * seq, d_model)
    for l in range(depth):
        w_all = w_in[l].astype(BF16)
        proj, gates, gates_t = _inproj(
            x2, attn_norm_w[l][None, :], w_all, w_all[:, gate_hi:],
            _pad_lanes(a_log[l], n_heads), _pad_lanes(dt_bias[l], n_heads),
            q_norm_w[l][None, :], k_norm_w[l][None, :], cos_t, sin_t, n_heads)
        o_a = _delta(proj, gates, gates_t, conv_w[l], delta_out_norm_w[l][None, :], bsz, seq, n_heads)
        o_b = _attention(proj, attn_out_norm_w[l][None, :], bsz, seq, n_heads, 4 * n_heads)
        x1, h2 = _outproj(o_a, o_b, w_out[l].astype(BF16), x2, ffn_norm_w[l][None, :])
        x2 = _ffn(h2, w_gate_up[l].astype(BF16), w_down[l].astype(BF16), x1)
    return x2.reshape(bsz, seq, d_model)
```

```python
import functools

import jax
import jax.numpy as jnp
from jax import lax
from jax.experimental import pallas as pl
from jax.experimental.pallas import tpu as pltpu

HEAD_DIM = 128
CONV_WIDTH = 4
DILATED_PATTERNS = ((128, 1), (512, 4), (2048, 16))
ROPE_THETA = 10000.0
EPS = 1e-6

LANES = 128
SUBLANES = 8
DELTA_CHUNK = 128
NEG_BIG = -1e30
ATTN_UNROLL = 8

F32 = jnp.float32
BF16 = jnp.bfloat16

VMEM_LIMIT = 60 * 1024 * 1024


def _mm(a, b):
    return jnp.dot(a.astype(BF16), b.astype(BF16), preferred_element_type=F32)


def _mm_nt(a, b):
    return lax.dot_general(a.astype(BF16), b.astype(BF16), (((1,), (1,)), ((), ())),
                           preferred_element_type=F32)


def _split3(x):
    hi = x.astype(BF16)
    rest = x - hi.astype(F32)
    mid = rest.astype(BF16)
    lo = (rest - mid.astype(F32)).astype(BF16)
    return jnp.concatenate([hi, mid, lo], axis=1)


def _silu(x):
    return x * jax.nn.sigmoid(x)


def _rope_table_kernel(pos_ref, invf_ref, cos_ref, sin_ref):
    ang = pos_ref[...].astype(F32) * invf_ref[...]
    lane = lax.broadcasted_iota(jnp.int32, ang.shape, 1)
    cos_ref[...] = jnp.cos(ang)
    s = jnp.sin(ang)
    sin_ref[...] = jnp.where(lane < HEAD_DIM // 2, -s, s)


def _rope_tables(positions):
    tokens = positions.size
    tm = 1024
    half = HEAD_DIM // 2
    inv_freq = ROPE_THETA ** (-jnp.arange(half, dtype=F32) / half)
    invf = jnp.concatenate([inv_freq, inv_freq])[None, :]
    return pl.pallas_call(
        _rope_table_kernel,
        grid=(tokens // tm,),
        in_specs=[pl.BlockSpec((tm, 1), lambda i: (i, 0)),
                  pl.BlockSpec((1, HEAD_DIM), lambda i: (0, 0))],
        out_specs=[pl.BlockSpec((tm, HEAD_DIM), lambda i: (i, 0))] * 2,
        out_shape=[jax.ShapeDtypeStruct((tokens, HEAD_DIM), F32)] * 2,
        name="rope_tables",
    )(positions.reshape(tokens, 1), invf)


def _inproj_kernel(x_ref, nw_ref, wd_ref, wa_ref, wg_ref, alog_ref, dtb_ref, qnw_ref,
                   knw_ref, cos_ref, sin_ref, out_ref, gates_ref, gates_t_ref, hn_ref, *, n_heads,
                   n_delta_tiles):
    j = pl.program_id(1)

    @pl.when(j == 0)
    def _():
        x = x_ref[...]
        y = x * lax.rsqrt(jnp.mean(x * x, axis=-1, keepdims=True) + EPS) * nw_ref[...]
        hn = y.astype(BF16)
        hn_ref[...] = hn
        gl = jnp.dot(hn, wg_ref[...], preferred_element_type=F32)
        lane = lax.broadcasted_iota(jnp.int32, gl.shape, 1)
        beta = jax.nn.sigmoid(gl)
        t = gl + dtb_ref[...]
        softplus = jnp.maximum(t, 0.0) + jnp.log1p(jnp.exp(-jnp.abs(t)))
        g = -jnp.exp(alog_ref[...]) * softplus
        c = DELTA_CHUNK
        tri = (lax.broadcasted_iota(jnp.int32, (c, c), 0)
               >= lax.broadcasted_iota(jnp.int32, (c, c), 1)).astype(BF16)
        g3 = _split3(g)
        parts = jnp.concatenate(
            [jnp.dot(tri, g3[ch * c:(ch + 1) * c], preferred_element_type=F32)
             for ch in range(g.shape[0] // c)], axis=0)
        gc = parts[:, :LANES] + parts[:, LANES:2 * LANES] + parts[:, 2 * LANES:]
        gates = jnp.where(lane < n_heads, beta, gc)
        gates_ref[...] = gates
        gates_t_ref[...] = gates.T[:2 * n_heads, :]

    @pl.when(j < n_delta_tiles)
    def _():
        out_ref[...] = jnp.dot(hn_ref[...], wd_ref[...],
                               preferred_element_type=F32).astype(out_ref.dtype)

    @pl.when(j == n_delta_tiles + 2)
    def _():
        out_ref[...] = jnp.dot(hn_ref[...], wa_ref[...],
                               preferred_element_type=F32).astype(out_ref.dtype)

    @pl.when(jnp.logical_or(j == n_delta_tiles, j == n_delta_tiles + 1))
    def _():
        acc = jnp.dot(hn_ref[...], wa_ref[...], preferred_element_type=F32)
        nw = jnp.where(j == n_delta_tiles, qnw_ref[...], knw_ref[...])
        out_scale = jnp.where(j == n_delta_tiles, HEAD_DIM ** -0.5, 1.0)
        half = HEAD_DIM // 2
        nw_rot = jnp.concatenate([nw[:, half:], nw[:, :half]], axis=1)
        cos_w = cos_ref[...] * (nw * out_scale)
        sin_w = sin_ref[...] * (nw_rot * out_scale)
        pi = lax.broadcasted_iota(jnp.int32, (HEAD_DIM, HEAD_DIM), 0)
        pj = lax.broadcasted_iota(jnp.int32, (HEAD_DIM, HEAD_DIM), 1)
        ones_m = jnp.ones((HEAD_DIM, HEAD_DIM), BF16)
        rot_m = (pi == ((pj + half) % HEAD_DIM)).astype(BF16)
        for h in range(n_heads):
            xh = acc[:, h * HEAD_DIM:(h + 1) * HEAD_DIM]
            x_rot = jnp.dot(xh.astype(BF16), rot_m, preferred_element_type=F32)
            mean_sq = jnp.dot((xh * xh).astype(BF16), ones_m,
                              preferred_element_type=F32) * (1.0 / HEAD_DIM)
            roped = lax.rsqrt(mean_sq + EPS) * (xh * cos_w + x_rot * sin_w)
            out_ref[:, h * HEAD_DIM:(h + 1) * HEAD_DIM] = roped.astype(out_ref.dtype)


def _inproj(x2, nw, w_all, w_attn, alog, dtb, qnw, knw, cos_t, sin_t, n_heads):
    tokens, d_model = x2.shape
    width = n_heads * HEAD_DIM
    n_delta_tiles = 4
    n_tiles = n_delta_tiles + w_attn.shape[1] // width
    tm = 1024
    kern = functools.partial(_inproj_kernel, n_heads=n_heads, n_delta_tiles=n_delta_tiles)
    row = lambda i, j: (i, 0)
    const = lambda i, j: (0, 0)
    once = dict(pipeline_mode=pl.Buffered(1))
    return pl.pallas_call(
        kern,
        grid=(tokens // tm, n_tiles),
        in_specs=[pl.BlockSpec((tm, d_model), row),
                  pl.BlockSpec((1, d_model), const),
                  pl.BlockSpec((d_model, width), lambda i, j: (0, jnp.minimum(j, n_delta_tiles - 1))),
                  pl.BlockSpec((d_model, width), lambda i, j: (0, jnp.maximum(j - n_delta_tiles, 0))),
                  pl.BlockSpec((d_model, LANES), lambda i, j: (0, n_delta_tiles * width // LANES), **once),
                  pl.BlockSpec((1, LANES), const),
                  pl.BlockSpec((1, LANES), const),
                  pl.BlockSpec((1, HEAD_DIM), const),
                  pl.BlockSpec((1, HEAD_DIM), const),
                  pl.BlockSpec((tm, HEAD_DIM), row),
                  pl.BlockSpec((tm, HEAD_DIM), row)],
        out_specs=[pl.BlockSpec((tm, width), lambda i, j: (i, j)),
                   pl.BlockSpec((tm, LANES), row),
                   pl.BlockSpec((2 * n_heads, tm), lambda i, j: (0, i))],
        out_shape=[jax.ShapeDtypeStruct((tokens, n_tiles * width), BF16),
                   jax.ShapeDtypeStruct((tokens, LANES), F32),
                   jax.ShapeDtypeStruct((2 * n_heads, tokens), F32)],
        scratch_shapes=[pltpu.VMEM((tm, d_model), BF16)],
        compiler_params=pltpu.CompilerParams(
            dimension_semantics=("parallel", "arbitrary"), vmem_limit_bytes=VMEM_LIMIT),
        name="inproj",
    )(x2, nw, w_all, w_attn, w_all, alog, dtb, qnw, knw, cos_t, sin_t)


def _delta_kernel(q_ref, k_ref, v_ref, z_ref, gates_ref, gates_t_ref, cwq_ref, cwk_ref, cwv_ref,
                  onw_ref, o_ref, state_ref, carry_ref, xpad_ref, *, n_heads, heads_per_step):
    hg = pl.program_id(1)
    s_idx = pl.program_id(2)
    ts = q_ref.shape[0]
    c = DELTA_CHUNK
    n_chunks = ts // c
    hd = HEAD_DIM

    @pl.when(s_idx == 0)
    def _():
        state_ref[...] = jnp.zeros_like(state_ref)
        carry_ref[...] = jnp.zeros_like(carry_ref)

    def conv_silu(x_ref, cw_ref, slot):
        w = cw_ref[...]
        halves = []
        for hh in range(heads_per_step):
            cols = slice(hh * hd, (hh + 1) * hd)
            xpad_ref[hh, pl.ds(0, SUBLANES, stride=2), :] = carry_ref[slot, :, cols]
            xpad_ref[hh, pl.ds(2 * SUBLANES, ts, stride=2), :] = x_ref[:, cols].astype(F32)
            y = jnp.zeros((ts, hd), F32)
            for j in range(CONV_WIDTH):
                off = 2 * (SUBLANES - (CONV_WIDTH - 1) + j)
                y = y + xpad_ref[hh, pl.ds(off, ts, stride=2), :] * w[j:j + 1, cols]
            halves.append(_silu(y))
            carry_ref[slot, :, cols] = xpad_ref[hh, pl.ds(2 * ts, SUBLANES, stride=2), :]
        return halves

    def l2n(t):
        return t * lax.rsqrt(jnp.sum(t * t, axis=-1, keepdims=True) + EPS)

    q_conv = conv_silu(q_ref, cwq_ref, 0)
    k_conv = conv_silu(k_ref, cwk_ref, 1)
    v_conv = conv_silu(v_ref, cwv_ref, 2)
    gt = gates_ref[...]
    lane = lax.broadcasted_iota(jnp.int32, gt.shape, 1)
    head_iota = lax.broadcasted_iota(jnp.int32, (n_heads, c), 0)

    ri = lax.broadcasted_iota(jnp.int32, (c, c), 0)
    ci = lax.broadcasted_iota(jnp.int32, (c, c), 1)
    incl = ri >= ci
    strict = ri > ci
    eye = (ri == ci).astype(F32)
    onw = onw_ref[...]

    qs, ks, vs, betas, gcbs, gcrs = [], [], [], [], [], []
    for hh in range(heads_per_step):
        head = hg * heads_per_step + hh
        q_h = l2n(q_conv[hh]) * (hd ** -0.5)
        k_h = l2n(k_conv[hh])
        v_h = v_conv[hh]
        beta_h = jnp.broadcast_to(
            jnp.sum(jnp.where(lane == head, gt, 0.0), axis=-1, keepdims=True), (ts, c))
        gc_h = jnp.broadcast_to(
            jnp.sum(jnp.where(lane == head + n_heads, gt, 0.0), axis=-1, keepdims=True), (ts, c))
        for ch in range(n_chunks):
            r = slice(ch * c, (ch + 1) * c)
            qs.append(q_h[r])
            ks.append(k_h[r])
            vs.append(v_h[r])
            betas.append(beta_h[r])
            gcbs.append(gc_h[r])
            gc_rows = gates_t_ref[n_heads:2 * n_heads, r]
            gcrs.append(jnp.sum(jnp.where(head_iota == head, gc_rows, 0.0),
                                axis=0, keepdims=True))
    units = range(heads_per_step * n_chunks)

    kks = [_mm_nt(ks[n], ks[n]) for n in units]
    qks = [_mm_nt(qs[n], ks[n]) for n in units]
    decays = [jnp.where(incl, jnp.exp(jnp.where(incl, gcbs[n] - gcrs[n], 0.0)), 0.0) for n in units]
    lowers = [jnp.where(strict, betas[n] * kks[n] * decays[n], 0.0) for n in units]
    same2 = (ri // 2) == (ci // 2)
    invs = [eye - jnp.where(same2, lowers[n], 0.0) for n in units]
    lowers_bf = [lowers[n].astype(BF16) for n in units]
    b = 2
    while b < c:
        off = jnp.logical_and((ri // (2 * b)) == (ci // (2 * b)), (ri // b) != (ci // b))
        off_bf = off.astype(F32).astype(BF16)
        tmp = [_mm(lowers_bf[n] * off_bf, invs[n]) for n in units]
        invs = [invs[n] - _mm(invs[n], tmp[n]) for n in units]
        b *= 2
    egcs = [jnp.exp(gcbs[n]) for n in units]
    us = [_mm(invs[n], vs[n] * betas[n]) for n in units]
    ws = [_mm(invs[n], ks[n] * (betas[n] * egcs[n])) for n in units]
    qks = [qks[n] * decays[n] for n in units]
    g_lasts = [gcbs[n][c - 1:c, :] for n in units]
    kd_ts = [(ks[n] * jnp.exp(g_lasts[n] - gcbs[n])).T for n in units]
    s_mix = [_mm(kd_ts[n], ws[n]) for n in units]
    s_add = [_mm(kd_ts[n], us[n]) for n in units]
    q_eff = [qs[n] * egcs[n] - _mm(qks[n], ws[n]) for n in units]
    o_loc = [_mm(qks[n], us[n]) for n in units]

    states = [state_ref[hh] for hh in range(heads_per_step)]
    outs = {}
    for ch in range(n_chunks):
        for hh in range(heads_per_step):
            n = hh * n_chunks + ch
            outs[n] = _mm(q_eff[n], states[hh]) + o_loc[n]
            states[hh] = (states[hh] * jnp.exp(g_lasts[n]) - _mm(s_mix[n], states[hh])) + s_add[n]
    for hh in range(heads_per_step):
        state_ref[hh] = states[hh]

    for hh in range(heads_per_step):
        cols = slice(hh * hd, (hh + 1) * hd)
        for ch in range(n_chunks):
            r = slice(ch * c, (ch + 1) * c)
            o_c = outs[hh * n_chunks + ch]
            o_n = o_c * lax.rsqrt(jnp.mean(o_c * o_c, axis=-1, keepdims=True) + EPS) * onw
            o_ref[r, cols] = (o_n * _silu(z_ref[r, cols].astype(F32))).astype(o_ref.dtype)


def _delta(proj, gates, gates_t, conv_w, onw, bsz, seq, n_heads):
    ts = 1024
    hb = 4
    s_tiles = seq // ts
    groups = n_heads // hb
    kern = functools.partial(_delta_kernel, n_heads=n_heads, heads_per_step=hb)

    def col(seg):
        return lambda b, g, s: (b * s_tiles + s, seg * groups + g)

    def cw(seg):
        return lambda b, g, s: (0, seg * groups + g)

    tile = (ts, hb * HEAD_DIM)
    return pl.pallas_call(
        kern,
        grid=(bsz, groups, s_tiles),
        in_specs=[pl.BlockSpec(tile, col(0)), pl.BlockSpec(tile, col(1)),
                  pl.BlockSpec(tile, col(2)), pl.BlockSpec(tile, col(3)),
                  pl.BlockSpec((ts, LANES), lambda b, g, s: (b * s_tiles + s, 0)),
                  pl.BlockSpec((2 * n_heads, ts), lambda b, g, s: (0, b * s_tiles + s)),
                  pl.BlockSpec((CONV_WIDTH, hb * HEAD_DIM), cw(0)),
                  pl.BlockSpec((CONV_WIDTH, hb * HEAD_DIM), cw(1)),
                  pl.BlockSpec((CONV_WIDTH, hb * HEAD_DIM), cw(2)),
                  pl.BlockSpec((1, HEAD_DIM), lambda b, g, s: (0, 0))],
        out_specs=pl.BlockSpec(tile, lambda b, g, s: (b * s_tiles + s, g)),
        out_shape=jax.ShapeDtypeStruct((bsz * seq, n_heads * HEAD_DIM), BF16),
        scratch_shapes=[pltpu.VMEM((hb, HEAD_DIM, HEAD_DIM), F32),
                        pltpu.VMEM((3, SUBLANES, hb * HEAD_DIM), F32),
                        pltpu.VMEM((hb, 2 * (ts + SUBLANES), HEAD_DIM), F32)],
        compiler_params=pltpu.CompilerParams(
            dimension_semantics=("parallel", "parallel", "arbitrary"),
            vmem_limit_bytes=VMEM_LIMIT),
        name="delta_rule",
    )(proj, proj, proj, proj, gates, gates_t, conv_w, conv_w, conv_w, onw)


def _attn_kernel(q_ref, k_ref, v_ref, onw_ref, o_ref, qf_ref, kf_ref, vf_ref, acc_ref, m_ref, l_ref):
    seq = q_ref.shape[0]
    hd = HEAD_DIM
    onw = onw_ref[...]
    order = sorted(DILATED_PATTERNS, key=lambda p: -p[1])
    qf_ref[...] = q_ref[...].astype(F32)
    kf_ref[...] = k_ref[...].astype(F32)
    vf_ref[...] = v_ref[...].astype(F32)

    for pi, (window, dil) in enumerate(order):
        first, last = pi == 0, pi == len(order) - 1
        blk = window // dil
        nb = seq // (blk * dil)
        run_len = min(nb, ATTN_UNROLL)
        runs_per_iter = ATTN_UNROLL // run_len
        runs_per_residue = nb // run_len
        qi = lax.broadcasted_iota(jnp.int32, (blk, blk), 0)
        ki = lax.broadcasted_iota(jnp.int32, (blk, blk), 1)
        cur_mask = ki <= qi
        prev_mask = ki >= qi
        ones_cols = jnp.ones((blk, hd), BF16)

        def body(it, carry, first=first, last=last, dil=dil, blk=blk, run_len=run_len,
                 runs_per_iter=runs_per_iter, runs_per_residue=runs_per_residue,
                 cur_mask=cur_mask, prev_mask=prev_mask, ones_cols=ones_cols):
            def rows(start):
                if dil == 1:
                    return pl.ds(pl.multiple_of(start, blk), blk)
                return pl.ds(start, blk, stride=dil)

            def tile(f32_ref, bf16_ref, start):
                if dil == 1:
                    return bf16_ref[rows(start), :]
                return f32_ref[rows(start), :].astype(BF16)

            blocks = []
            for rn in range(runs_per_iter):
                run = it * runs_per_iter + rn
                if runs_per_residue == 1:
                    r, j0 = run, 0
                    prev_k = prev_v = prev_ok = None
                else:
                    r = run // runs_per_residue
                    j0 = (run % runs_per_residue) * run_len
                    pstart = jnp.maximum(j0 - 1, 0) * (blk * dil) + r
                    prev_k, prev_v = tile(kf_ref, k_ref, pstart), tile(vf_ref, v_ref, pstart)
                    prev_ok = j0 > 0
                for jj in range(run_len):
                    start = (j0 + jj) * (blk * dil) + r
                    cur = rows(start)
                    k_t, v_t = tile(kf_ref, k_ref, start), tile(vf_ref, v_ref, start)
                    old = None if first else (m_ref[cur, :], l_ref[cur, :], acc_ref[cur, :])
                    blocks.append((cur, tile(qf_ref, q_ref, start), k_t, v_t, prev_k, prev_v,
                                   prev_ok, old))
                    prev_k, prev_v, prev_ok = k_t, v_t, None

            scores = []
            for cur, q_t, k_t, v_t, prev_k, prev_v, prev_ok, old in blocks:
                if prev_k is None:
                    scores.append((jnp.where(cur_mask, _mm_nt(q_t, k_t), NEG_BIG), None))
                else:
                    s2 = _mm_nt(q_t, jnp.concatenate([k_t, prev_k], axis=0))
                    pm = prev_mask if prev_ok is None else jnp.logical_and(prev_mask, prev_ok)
                    scores.append((jnp.where(cur_mask, s2[:, :blk], NEG_BIG),
                                   jnp.where(pm, s2[:, blk:], NEG_BIG)))

            probs = []
            for (cur, q_t, k_t, v_t, prev_k, prev_v, prev_ok, old), (s_cur, s_prev) in zip(blocks, scores):
                s_max = s_cur if s_prev is None else jnp.maximum(s_cur, s_prev)
                m_new = jnp.max(s_max, axis=-1, keepdims=True)
                if old is not None:
                    m_new = jnp.maximum(old[0], m_new)
                p_cur = jnp.exp(s_cur - m_new).astype(BF16)
                p_prev = None if s_prev is None else jnp.exp(s_prev - m_new).astype(BF16)
                probs.append((m_new, p_cur, p_prev))

            results = []
            for (cur, q_t, k_t, v_t, prev_k, prev_v, prev_ok, old), (m_new, p_cur, p_prev) in zip(blocks, probs):
                v_ext = jnp.concatenate([v_t, ones_cols], axis=1)
                if p_prev is None:
                    pv = jnp.dot(p_cur, v_ext, preferred_element_type=F32)
                else:
                    pv = jnp.dot(jnp.concatenate([p_cur, p_prev], axis=1),
                                 jnp.concatenate(
                                     [v_ext, jnp.concatenate([prev_v, ones_cols], axis=1)], axis=0),
                                 preferred_element_type=F32)
                acc_new, l_new = pv[:, :hd], pv[:, hd:]
                if old is not None:
                    alpha = jnp.exp(old[0] - m_new)
                    l_new = alpha * old[1] + l_new
                    acc_new = alpha * old[2] + acc_new
                results.append((cur, m_new, l_new, acc_new))

            for cur, m_new, l_new, acc_new in results:
                if last:
                    o = acc_new / l_new
                    o = o * lax.rsqrt(jnp.mean(o * o, axis=-1, keepdims=True) + EPS) * onw
                    o_ref[cur, :] = o.astype(o_ref.dtype)
                else:
                    m_ref[cur, :] = jnp.broadcast_to(m_new, (blk, hd))
                    l_ref[cur, :] = l_new
                    acc_ref[cur, :] = acc_new
            return carry

        lax.fori_loop(0, dil * nb // ATTN_UNROLL, body, 0)


def _attention(proj, onw, bsz, seq, n_heads, col0):
    tile = (seq, HEAD_DIM)

    def col(seg):
        return lambda b, h: (b, col0 + seg * n_heads + h)

    return pl.pallas_call(
        _attn_kernel,
        grid=(bsz, n_heads),
        in_specs=[pl.BlockSpec(tile, col(0)), pl.BlockSpec(tile, col(1)),
                  pl.BlockSpec(tile, col(2)),
                  pl.BlockSpec((1, HEAD_DIM), lambda b, h: (0, 0))],
        out_specs=pl.BlockSpec(tile, lambda b, h: (b, h)),
        out_shape=jax.ShapeDtypeStruct((bsz * seq, n_heads * HEAD_DIM), BF16),
        scratch_shapes=[pltpu.VMEM(tile, F32)] * 6,
        compiler_params=pltpu.CompilerParams(
            dimension_semantics=("parallel", "parallel"), vmem_limit_bytes=VMEM_LIMIT),
        name="dilated_attention",
    )(proj, proj, proj, onw)


def _outproj_kernel(oa_ref, ob_ref, wa_ref, wb_ref, x_ref, fnw_ref, x1_ref, h2_ref):
    acc = (jnp.dot(oa_ref[...], wa_ref[...], preferred_element_type=F32)
           + jnp.dot(ob_ref[...], wb_ref[...], preferred_element_type=F32))
    x1 = x_ref[...] + acc
    x1_ref[...] = x1
    h2 = x1 * lax.rsqrt(jnp.mean(x1 * x1, axis=-1, keepdims=True) + EPS) * fnw_ref[...]
    h2_ref[...] = h2.astype(h2_ref.dtype)


def _outproj(o_a, o_b, w_out, x2, fnw):
    tokens, d_model = x2.shape
    wa_rows = o_a.shape[1]
    tm = 512
    row = lambda i: (i, 0)
    return pl.pallas_call(
        _outproj_kernel,
        grid=(tokens // tm,),
        in_specs=[pl.BlockSpec((tm, wa_rows), row),
                  pl.BlockSpec((tm, o_b.shape[1]), row),
                  pl.BlockSpec((wa_rows, d_model), lambda i: (0, 0)),
                  pl.BlockSpec((o_b.shape[1], d_model), lambda i: (1, 0)),
                  pl.BlockSpec((tm, d_model), row),
                  pl.BlockSpec((1, d_model), lambda i: (0, 0))],
        out_specs=[pl.BlockSpec((tm, d_model), row), pl.BlockSpec((tm, d_model), row)],
        out_shape=[jax.ShapeDtypeStruct((tokens, d_model), F32),
                   jax.ShapeDtypeStruct((tokens, d_model), BF16)],
        compiler_params=pltpu.CompilerParams(
            dimension_semantics=("parallel",), vmem_limit_bytes=VMEM_LIMIT),
        name="outproj",
    )(o_a, o_b, w_out, w_out, x2, fnw)


def _ffn_kernel(h_ref, wg_ref, wu_ref, wd_ref, x1_ref, o_ref):
    @pl.when(pl.program_id(1) == 0)
    def _():
        o_ref[...] = x1_ref[...]

    h = h_ref[...]
    g = jnp.dot(h, wg_ref[...], preferred_element_type=F32)
    u = jnp.dot(h, wu_ref[...], preferred_element_type=F32)
    a = (_silu(g) * u).astype(BF16)
    o_ref[...] += jnp.dot(a, wd_ref[...], preferred_element_type=F32)


def _ffn(h2, w_gate_up, w_down, x1):
    tokens, d_model = x1.shape
    d_ff = w_down.shape[0]
    tm, tf = 1024, 512
    nf = d_ff // tf
    row = lambda i, f: (i, 0)
    return pl.pallas_call(
        _ffn_kernel,
        grid=(tokens // tm, nf),
        in_specs=[pl.BlockSpec((tm, d_model), row),
                  pl.BlockSpec((d_model, tf), lambda i, f: (0, f)),
                  pl.BlockSpec((d_model, tf), lambda i, f: (0, nf + f)),
                  pl.BlockSpec((tf, d_model), lambda i, f: (f, 0)),
                  pl.BlockSpec((tm, d_model), row)],
        out_specs=pl.BlockSpec((tm, d_model), row),
        out_shape=jax.ShapeDtypeStruct((tokens, d_model), F32),
        compiler_params=pltpu.CompilerParams(
            dimension_semantics=("parallel", "arbitrary"), vmem_limit_bytes=VMEM_LIMIT),
        name="swiglu",
    )(h2, w_gate_up, w_gate_up, w_down, x1)


def _pad_lanes(v, offset):
    out = jnp.zeros((1, LANES), F32)
    return lax.dynamic_update_slice(out, v.astype(F32)[None, :], (0, offset))


def kernel(x, positions, attn_norm_w, w_in, conv_w, a_log, dt_bias, delta_out_norm_w, q_norm_w,
           k_norm_w, attn_out_norm_w, w_out, ffn_norm_w, w_gate_up, w_down):
    bsz, seq, d_model = x.shape
    depth = w_in.shape[0]
    n_heads = a_log.shape[1]
    width = n_heads * HEAD_DIM
    gate_lo, gate_hi = 4 * width, 4 * width + 2 * n_heads
    assert 2 * n_heads <= LANES and w_in.shape[2] == gate_hi + 3 * width

    cos_t, sin_t = _rope_tables(positions)
    x2 = x.reshape(bsz * seq, d_model)
    for l in range(depth):
        w_delta = w_in[l][:, :gate_lo + LANES].astype(BF16)
        w_attn = w_in[l][:, gate_hi:].astype(BF16)
        proj, gates, gates_t = _inproj(
            x2, attn_norm_w[l][None, :], w_delta, w_attn,
            _pad_lanes(a_log[l], n_heads), _pad_lanes(dt_bias[l], n_heads),
            q_norm_w[l][None, :], k_norm_w[l][None, :], cos_t, sin_t, n_heads)
        o_a = _delta(proj, gates, gates_t, conv_w[l], delta_out_norm_w[l][None, :], bsz, seq, n_heads)
        o_b = _attention(proj, attn_out_norm_w[l][None, :], bsz, seq, n_heads, 4 * n_heads)
        x1, h2 = _outproj(o_a, o_b, w_out[l].astype(BF16), x2, ffn_norm_w[l][None, :])
        x2 = _ffn(h2, w_gate_up[l].astype(BF16), w_down[l].astype(BF16), x1)
    return x2.reshape(bsz, seq, d_model)
```

```python
import functools

import jax
import jax.numpy as jnp
from jax import lax
from jax.experimental import pallas as pl
from jax.experimental.pallas import tpu as pltpu

HEAD_DIM = 128
CONV_WIDTH = 4
DILATED_PATTERNS = ((128, 1), (512, 4), (2048, 16))
ROPE_THETA = 10000.0
EPS = 1e-6

LANES = 128
SUBLANES = 8
DELTA_CHUNK = 128
NEG_BIG = -1e30
ATTN_UNROLL = 8

F32 = jnp.float32
BF16 = jnp.bfloat16

VMEM_LIMIT = 60 * 1024 * 1024


def _mm(a, b):
    return jnp.dot(a.astype(BF16), b.astype(BF16), preferred_element_type=F32)


def _mm_nt(a, b):
    return lax.dot_general(a.astype(BF16), b.astype(BF16), (((1,), (1,)), ((), ())),
                           preferred_element_type=F32)


def _split3(x):
    hi = x.astype(BF16)
    rest = x - hi.astype(F32)
    mid = rest.astype(BF16)
    lo = (rest - mid.astype(F32)).astype(BF16)
    return jnp.concatenate([hi, mid, lo], axis=1)


def _silu(x):
    return x * jax.nn.sigmoid(x)


def _rope_table_kernel(pos_ref, invf_ref, cos_ref, sin_ref):
    ang = pos_ref[...].astype(F32) * invf_ref[...]
    lane = lax.broadcasted_iota(jnp.int32, ang.shape, 1)
    cos_ref[...] = jnp.cos(ang)
    s = jnp.sin(ang)
    sin_ref[...] = jnp.where(lane < HEAD_DIM // 2, -s, s)


def _rope_tables(positions):
    tokens = positions.size
    tm = 1024
    half = HEAD_DIM // 2
    inv_freq = ROPE_THETA ** (-jnp.arange(half, dtype=F32) / half)
    invf = jnp.concatenate([inv_freq, inv_freq])[None, :]
    return pl.pallas_call(
        _rope_table_kernel,
        grid=(tokens // tm,),
        in_specs=[pl.BlockSpec((tm, 1), lambda i: (i, 0)),
                  pl.BlockSpec((1, HEAD_DIM), lambda i: (0, 0))],
        out_specs=[pl.BlockSpec((tm, HEAD_DIM), lambda i: (i, 0))] * 2,
        out_shape=[jax.ShapeDtypeStruct((tokens, HEAD_DIM), F32)] * 2,
        name="rope_tables",
    )(positions.reshape(tokens, 1), invf)


def _inproj_kernel(x_ref, nw_ref, wd_ref, wa_ref, wg_ref, alog_ref, dtb_ref, qnw_ref,
                   knw_ref, cos_ref, sin_ref, ride_ref,
                   out_ref, gates_ref, gates_t_ref, cast_ref, hn_ref, *, n_heads, n_delta_tiles):
    j = pl.program_id(1)

    cast_ref[...] = ride_ref[...].astype(cast_ref.dtype)

    @pl.when(j == 0)
    def _():
        x = x_ref[...]
        y = x * lax.rsqrt(jnp.mean(x * x, axis=-1, keepdims=True) + EPS) * nw_ref[...]
        hn = y.astype(BF16)
        hn_ref[...] = hn
        gl = jnp.dot(hn, wg_ref[...], preferred_element_type=F32)
        lane = lax.broadcasted_iota(jnp.int32, gl.shape, 1)
        beta = jax.nn.sigmoid(gl)
        t = gl + dtb_ref[...]
        softplus = jnp.maximum(t, 0.0) + jnp.log1p(jnp.exp(-jnp.abs(t)))
        g = -jnp.exp(alog_ref[...]) * softplus
        c = DELTA_CHUNK
        tri = (lax.broadcasted_iota(jnp.int32, (c, c), 0)
               >= lax.broadcasted_iota(jnp.int32, (c, c), 1)).astype(BF16)
        g3 = _split3(g)
        parts = jnp.concatenate(
            [jnp.dot(tri, g3[ch * c:(ch + 1) * c], preferred_element_type=F32)
             for ch in range(g.shape[0] // c)], axis=0)
        gc = parts[:, :LANES] + parts[:, LANES:2 * LANES] + parts[:, 2 * LANES:]
        gates = jnp.where(lane < n_heads, beta, gc)
        gates_ref[...] = gates
        gates_t_ref[...] = gates.T[:2 * n_heads, :]

    @pl.when(j < n_delta_tiles)
    def _():
        out_ref[...] = jnp.dot(hn_ref[...], wd_ref[...],
                               preferred_element_type=F32).astype(out_ref.dtype)

    @pl.when(j == n_delta_tiles + 2)
    def _():
        out_ref[...] = jnp.dot(hn_ref[...], wa_ref[...],
                               preferred_element_type=F32).astype(out_ref.dtype)

    @pl.when(jnp.logical_or(j == n_delta_tiles, j == n_delta_tiles + 1))
    def _():
        acc = jnp.dot(hn_ref[...], wa_ref[...], preferred_element_type=F32)
        nw = jnp.where(j == n_delta_tiles, qnw_ref[...], knw_ref[...])
        out_scale = jnp.where(j == n_delta_tiles, HEAD_DIM ** -0.5, 1.0)
        half = HEAD_DIM // 2
        nw_rot = jnp.concatenate([nw[:, half:], nw[:, :half]], axis=1)
        cos_w = cos_ref[...] * (nw * out_scale)
        sin_w = sin_ref[...] * (nw_rot * out_scale)
        pi = lax.broadcasted_iota(jnp.int32, (HEAD_DIM, HEAD_DIM), 0)
        pj = lax.broadcasted_iota(jnp.int32, (HEAD_DIM, HEAD_DIM), 1)
        ones_m = jnp.ones((HEAD_DIM, HEAD_DIM), BF16)
        rot_m = (pi == ((pj + half) % HEAD_DIM)).astype(BF16)
        for h in range(n_heads):
            xh = acc[:, h * HEAD_DIM:(h + 1) * HEAD_DIM]
            x_rot = jnp.dot(xh.astype(BF16), rot_m, preferred_element_type=F32)
            mean_sq = jnp.dot((xh * xh).astype(BF16), ones_m,
                              preferred_element_type=F32) * (1.0 / HEAD_DIM)
            roped = lax.rsqrt(mean_sq + EPS) * (xh * cos_w + x_rot * sin_w)
            out_ref[:, h * HEAD_DIM:(h + 1) * HEAD_DIM] = roped.astype(out_ref.dtype)


def _ride_spec(shape, axis, grid):
    align = LANES if axis == 1 else 2 * SUBLANES
    n_steps = grid[0] * grid[1]
    dim = shape[axis]
    size = next(b for b in range(align, dim + 1, align) if dim % b == 0 and dim // b <= n_steps)
    n_blocks = dim // size
    block = tuple(size if a == axis else d for a, d in enumerate(shape))

    def index(i, j):
        t = jnp.minimum(i * grid[1] + j, n_blocks - 1)
        return tuple(t if a == axis else 0 for a in range(len(shape)))

    return pl.BlockSpec(block, index)


def _inproj(x2, nw, w_all, w_attn, alog, dtb, qnw, knw, cos_t, sin_t, n_heads, rider):
    tokens, d_model = x2.shape
    width = n_heads * HEAD_DIM
    n_delta_tiles = 4
    n_tiles = n_delta_tiles + w_attn.shape[1] // width
    tm = 1024
    grid = (tokens // tm, n_tiles)
    ride_specs = [_ride_spec(rider[0].shape, rider[1], grid)]
    kern = functools.partial(_inproj_kernel, n_heads=n_heads, n_delta_tiles=n_delta_tiles)
    row = lambda i, j: (i, 0)
    const = lambda i, j: (0, 0)
    once = dict(pipeline_mode=pl.Buffered(1))
    return pl.pallas_call(
        kern,
        grid=grid,
        in_specs=[pl.BlockSpec((tm, d_model), row),
                  pl.BlockSpec((1, d_model), const),
                  pl.BlockSpec((d_model, width), lambda i, j: (0, jnp.minimum(j, n_delta_tiles - 1))),
                  pl.BlockSpec((d_model, width), lambda i, j: (0, jnp.maximum(j - n_delta_tiles, 0))),
                  pl.BlockSpec((d_model, LANES), lambda i, j: (0, n_delta_tiles * width // LANES), **once),
                  pl.BlockSpec((1, LANES), const),
                  pl.BlockSpec((1, LANES), const),
                  pl.BlockSpec((1, HEAD_DIM), const),
                  pl.BlockSpec((1, HEAD_DIM), const),
                  pl.BlockSpec((tm, HEAD_DIM), row),
                  pl.BlockSpec((tm, HEAD_DIM), row)] + ride_specs,
        out_specs=[pl.BlockSpec((tm, width), lambda i, j: (i, j)),
                   pl.BlockSpec((tm, LANES), row),
                   pl.BlockSpec((2 * n_heads, tm), lambda i, j: (0, i))] + ride_specs,
        out_shape=[jax.ShapeDtypeStruct((tokens, n_tiles * width), BF16),
                   jax.ShapeDtypeStruct((tokens, LANES), F32),
                   jax.ShapeDtypeStruct((2 * n_heads, tokens), F32)]
                  + [jax.ShapeDtypeStruct(rider[0].shape, BF16)],
        scratch_shapes=[pltpu.VMEM((tm, d_model), BF16)],
        compiler_params=pltpu.CompilerParams(
            dimension_semantics=("arbitrary", "arbitrary"), vmem_limit_bytes=VMEM_LIMIT),
        name="inproj",
    )(x2, nw, w_all, w_attn, w_all, alog, dtb, qnw, knw, cos_t, sin_t, rider[0])


def _delta_kernel(q_ref, k_ref, v_ref, z_ref, gates_ref, gates_t_ref, cwq_ref, cwk_ref, cwv_ref,
                  onw_ref, o_ref, state_ref, carry_ref, xpad_ref, *, n_heads, heads_per_step):
    hg = pl.program_id(1)
    s_idx = pl.program_id(2)
    ts = q_ref.shape[0]
    c = DELTA_CHUNK
    n_chunks = ts // c
    hd = HEAD_DIM

    @pl.when(s_idx == 0)
    def _():
        state_ref[...] = jnp.zeros_like(state_ref)
        carry_ref[...] = jnp.zeros_like(carry_ref)

    def conv_silu(x_ref, cw_ref, slot):
        w = cw_ref[...]
        halves = []
        for hh in range(heads_per_step):
            cols = slice(hh * hd, (hh + 1) * hd)
            xpad_ref[hh, pl.ds(0, SUBLANES, stride=2), :] = carry_ref[slot, :, cols]
            xpad_ref[hh, pl.ds(2 * SUBLANES, ts, stride=2), :] = x_ref[:, cols].astype(F32)
            y = jnp.zeros((ts, hd), F32)
            for j in range(CONV_WIDTH):
                off = 2 * (SUBLANES - (CONV_WIDTH - 1) + j)
                y = y + xpad_ref[hh, pl.ds(off, ts, stride=2), :] * w[j:j + 1, cols]
            halves.append(_silu(y))
            carry_ref[slot, :, cols] = xpad_ref[hh, pl.ds(2 * ts, SUBLANES, stride=2), :]
        return halves

    def l2n(t):
        return t * lax.rsqrt(jnp.sum(t * t, axis=-1, keepdims=True) + EPS)

    q_conv = conv_silu(q_ref, cwq_ref, 0)
    k_conv = conv_silu(k_ref, cwk_ref, 1)
    v_conv = conv_silu(v_ref, cwv_ref, 2)
    gt = gates_ref[...]
    lane = lax.broadcasted_iota(jnp.int32, gt.shape, 1)
    head_iota = lax.broadcasted_iota(jnp.int32, (n_heads, c), 0)

    ri = lax.broadcasted_iota(jnp.int32, (c, c), 0)
    ci = lax.broadcasted_iota(jnp.int32, (c, c), 1)
    incl = ri >= ci
    strict = ri > ci
    eye = (ri == ci).astype(F32)
    onw = onw_ref[...]

    qs, ks, vs, betas, gcbs, gcrs = [], [], [], [], [], []
    for hh in range(heads_per_step):
        head = hg * heads_per_step + hh
        q_h = l2n(q_conv[hh]) * (hd ** -0.5)
        k_h = l2n(k_conv[hh])
        v_h = v_conv[hh]
        beta_h = jnp.broadcast_to(
            jnp.sum(jnp.where(lane == head, gt, 0.0), axis=-1, keepdims=True), (ts, c))
        gc_h = jnp.broadcast_to(
            jnp.sum(jnp.where(lane == head + n_heads, gt, 0.0), axis=-1, keepdims=True), (ts, c))
        for ch in range(n_chunks):
            r = slice(ch * c, (ch + 1) * c)
            qs.append(q_h[r])
            ks.append(k_h[r])
            vs.append(v_h[r])
            betas.append(beta_h[r])
            gcbs.append(gc_h[r])
            gc_rows = gates_t_ref[n_heads:2 * n_heads, r]
            gcrs.append(jnp.sum(jnp.where(head_iota == head, gc_rows, 0.0),
                                axis=0, keepdims=True))
    units = range(heads_per_step * n_chunks)

    kks = [_mm_nt(ks[n], ks[n]) for n in units]
    qks = [_mm_nt(qs[n], ks[n]) for n in units]
    decays = [jnp.where(incl, jnp.exp(jnp.where(incl, gcbs[n] - gcrs[n], 0.0)), 0.0) for n in units]
    lowers = [jnp.where(strict, betas[n] * kks[n] * decays[n], 0.0) for n in units]
    same2 = (ri // 2) == (ci // 2)
    invs = [eye - jnp.where(same2, lowers[n], 0.0) for n in units]
    lowers_bf = [lowers[n].astype(BF16) for n in units]
    b = 2
    while b < c:
        off = jnp.logical_and((ri // (2 * b)) == (ci // (2 * b)), (ri // b) != (ci // b))
        off_bf = off.astype(F32).astype(BF16)
        tmp = [_mm(lowers_bf[n] * off_bf, invs[n]) for n in units]
        invs = [invs[n] - _mm(invs[n], tmp[n]) for n in units]
        b *= 2
    egcs = [jnp.exp(gcbs[n]) for n in units]
    us = [_mm(invs[n], vs[n] * betas[n]) for n in units]
    ws = [_mm(invs[n], ks[n] * (betas[n] * egcs[n])) for n in units]
    qks = [qks[n] * decays[n] for n in units]
    g_lasts = [gcbs[n][c - 1:c, :] for n in units]
    kd_ts = [(ks[n] * jnp.exp(g_lasts[n] - gcbs[n])).T for n in units]
    s_mix = [_mm(kd_ts[n], ws[n]) for n in units]
    s_add = [_mm(kd_ts[n], us[n]) for n in units]
    q_eff = [qs[n] * egcs[n] - _mm(qks[n], ws[n]) for n in units]
    o_loc = [_mm(qks[n], us[n]) for n in units]

    states = [state_ref[hh] for hh in range(heads_per_step)]
    outs = {}
    for ch in range(n_chunks):
        for hh in range(heads_per_step):
            n = hh * n_chunks + ch
            outs[n] = _mm(q_eff[n], states[hh]) + o_loc[n]
            states[hh] = (states[hh] * jnp.exp(g_lasts[n]) - _mm(s_mix[n], states[hh])) + s_add[n]
    for hh in range(heads_per_step):
        state_ref[hh] = states[hh]

    for hh in range(heads_per_step):
        cols = slice(hh * hd, (hh + 1) * hd)
        for ch in range(n_chunks):
            r = slice(ch * c, (ch + 1) * c)
            o_c = outs[hh * n_chunks + ch]
            o_n = o_c * lax.rsqrt(jnp.mean(o_c * o_c, axis=-1, keepdims=True) + EPS) * onw
            o_ref[r, cols] = (o_n * _silu(z_ref[r, cols].astype(F32))).astype(o_ref.dtype)


def _delta(proj, gates, gates_t, conv_w, onw, bsz, seq, n_heads):
    ts = 1024
    hb = 4
    s_tiles = seq // ts
    groups = n_heads // hb
    kern = functools.partial(_delta_kernel, n_heads=n_heads, heads_per_step=hb)

    def col(seg):
        return lambda b, g, s: (b * s_tiles + s, seg * groups + g)

    def cw(seg):
        return lambda b, g, s: (0, seg * groups + g)

    tile = (ts, hb * HEAD_DIM)
    return pl.pallas_call(
        kern,
        grid=(bsz, groups, s_tiles),
        in_specs=[pl.BlockSpec(tile, col(0)), pl.BlockSpec(tile, col(1)),
                  pl.BlockSpec(tile, col(2)), pl.BlockSpec(tile, col(3)),
                  pl.BlockSpec((ts, LANES), lambda b, g, s: (b * s_tiles + s, 0)),
                  pl.BlockSpec((2 * n_heads, ts), lambda b, g, s: (0, b * s_tiles + s)),
                  pl.BlockSpec((CONV_WIDTH, hb * HEAD_DIM), cw(0)),
                  pl.BlockSpec((CONV_WIDTH, hb * HEAD_DIM), cw(1)),
                  pl.BlockSpec((CONV_WIDTH, hb * HEAD_DIM), cw(2)),
                  pl.BlockSpec((1, HEAD_DIM), lambda b, g, s: (0, 0))],
        out_specs=pl.BlockSpec(tile, lambda b, g, s: (b * s_tiles + s, g)),
        out_shape=jax.ShapeDtypeStruct((bsz * seq, n_heads * HEAD_DIM), BF16),
        scratch_shapes=[pltpu.VMEM((hb, HEAD_DIM, HEAD_DIM), F32),
                        pltpu.VMEM((3, SUBLANES, hb * HEAD_DIM), F32),
                        pltpu.VMEM((hb, 2 * (ts + SUBLANES), HEAD_DIM), F32)],
        compiler_params=pltpu.CompilerParams(
            dimension_semantics=("parallel", "parallel", "arbitrary"),
            vmem_limit_bytes=VMEM_LIMIT),
        name="delta_rule",
    )(proj, proj, proj, proj, gates, gates_t, conv_w, conv_w, conv_w, onw)


def _attn_kernel(q_ref, k_ref, v_ref, onw_ref, ride0_ref, ride1_ref, o_ref, cast0_ref, cast1_ref,
                 qf_ref, kf_ref, vf_ref, acc_ref, m_ref, l_ref):
    seq = q_ref.shape[0]
    hd = HEAD_DIM
    onw = onw_ref[...]
    order = sorted(DILATED_PATTERNS, key=lambda p: -p[1])
    cast0_ref[...] = ride0_ref[...].astype(cast0_ref.dtype)
    cast1_ref[...] = ride1_ref[...].astype(cast1_ref.dtype)
    qf_ref[...] = q_ref[...].astype(F32)
    kf_ref[...] = k_ref[...].astype(F32)
    vf_ref[...] = v_ref[...].astype(F32)

    for pi, (window, dil) in enumerate(order):
        first, last = pi == 0, pi == len(order) - 1
        blk = window // dil
        nb = seq // (blk * dil)
        run_len = min(nb, ATTN_UNROLL)
        runs_per_iter = ATTN_UNROLL // run_len
        runs_per_residue = nb // run_len
        qi = lax.broadcasted_iota(jnp.int32, (blk, blk), 0)
        ki = lax.broadcasted_iota(jnp.int32, (blk, blk), 1)
        cur_mask = ki <= qi
        prev_mask = ki >= qi
        ones_cols = jnp.ones((blk, hd), BF16)

        def body(it, carry, first=first, last=last, dil=dil, blk=blk, run_len=run_len,
                 runs_per_iter=runs_per_iter, runs_per_residue=runs_per_residue,
                 cur_mask=cur_mask, prev_mask=prev_mask, ones_cols=ones_cols):
            def rows(start):
                if dil == 1:
                    return pl.ds(pl.multiple_of(start, blk), blk)
                return pl.ds(start, blk, stride=dil)

            def tile(f32_ref, bf16_ref, start):
                if dil == 1:
                    return bf16_ref[rows(start), :]
                return f32_ref[rows(start), :].astype(BF16)

            blocks = []
            for rn in range(runs_per_iter):
                run = it * runs_per_iter + rn
                if runs_per_residue == 1:
                    r, j0 = run, 0
                    prev_k = prev_v = prev_ok = None
                else:
                    r = run // runs_per_residue
                    j0 = (run % runs_per_residue) * run_len
                    pstart = jnp.maximum(j0 - 1, 0) * (blk * dil) + r
                    prev_k, prev_v = tile(kf_ref, k_ref, pstart), tile(vf_ref, v_ref, pstart)
                    prev_ok = j0 > 0
                for jj in range(run_len):
                    start = (j0 + jj) * (blk * dil) + r
                    cur = rows(start)
                    k_t, v_t = tile(kf_ref, k_ref, start), tile(vf_ref, v_ref, start)
                    old = None if first else (m_ref[cur, :], l_ref[cur, :], acc_ref[cur, :])
                    blocks.append((cur, tile(qf_ref, q_ref, start), k_t, v_t, prev_k, prev_v,
                                   prev_ok, old))
                    prev_k, prev_v, prev_ok = k_t, v_t, None

            scores = []
            for cur, q_t, k_t, v_t, prev_k, prev_v, prev_ok, old in blocks:
                if prev_k is None:
                    scores.append((jnp.where(cur_mask, _mm_nt(q_t, k_t), NEG_BIG), None))
                else:
                    s2 = _mm_nt(q_t, jnp.concatenate([k_t, prev_k], axis=0))
                    pm = prev_mask if prev_ok is None else jnp.logical_and(prev_mask, prev_ok)
                    scores.append((jnp.where(cur_mask, s2[:, :blk], NEG_BIG),
                                   jnp.where(pm, s2[:, blk:], NEG_BIG)))

            probs = []
            for (cur, q_t, k_t, v_t, prev_k, prev_v, prev_ok, old), (s_cur, s_prev) in zip(blocks, scores):
                s_max = s_cur if s_prev is None else jnp.maximum(s_cur, s_prev)
                m_new = jnp.max(s_max, axis=-1, keepdims=True)
                if old is not None:
                    m_new = jnp.maximum(old[0], m_new)
                p_cur = jnp.exp(s_cur - m_new).astype(BF16)
                p_prev = None if s_prev is None else jnp.exp(s_prev - m_new).astype(BF16)
                probs.append((m_new, p_cur, p_prev))

            results = []
            for (cur, q_t, k_t, v_t, prev_k, prev_v, prev_ok, old), (m_new, p_cur, p_prev) in zip(blocks, probs):
                v_ext = jnp.concatenate([v_t, ones_cols], axis=1)
                if p_prev is None:
                    pv = jnp.dot(p_cur, v_ext, preferred_element_type=F32)
                else:
                    pv = jnp.dot(jnp.concatenate([p_cur, p_prev], axis=1),
                                 jnp.concatenate(
                                     [v_ext, jnp.concatenate([prev_v, ones_cols], axis=1)], axis=0),
                                 preferred_element_type=F32)
                acc_new, l_new = pv[:, :hd], pv[:, hd:]
                if old is not None:
                    alpha = jnp.exp(old[0] - m_new)
                    l_new = alpha * old[1] + l_new
                    acc_new = alpha * old[2] + acc_new
                results.append((cur, m_new, l_new, acc_new))

            for cur, m_new, l_new, acc_new in results:
                if last:
                    o = acc_new / l_new
                    o = o * lax.rsqrt(jnp.mean(o * o, axis=-1, keepdims=True) + EPS) * onw
                    o_ref[cur, :] = o.astype(o_ref.dtype)
                else:
                    m_ref[cur, :] = jnp.broadcast_to(m_new, (blk, hd))
                    l_ref[cur, :] = l_new
                    acc_ref[cur, :] = acc_new
            return carry

        lax.fori_loop(0, dil * nb // ATTN_UNROLL, body, 0)


def _attention(proj, onw, bsz, seq, n_heads, col0, riders):
    tile = (seq, HEAD_DIM)
    grid = (bsz, n_heads)
    ride_specs = [_ride_spec(w.shape, axis, grid) for w, axis in riders]

    def col(seg):
        return lambda b, h: (b, col0 + seg * n_heads + h)

    return pl.pallas_call(
        _attn_kernel,
        grid=grid,
        in_specs=[pl.BlockSpec(tile, col(0)), pl.BlockSpec(tile, col(1)),
                  pl.BlockSpec(tile, col(2)),
                  pl.BlockSpec((1, HEAD_DIM), lambda b, h: (0, 0))] + ride_specs,
        out_specs=[pl.BlockSpec(tile, lambda b, h: (b, h))] + ride_specs,
        out_shape=[jax.ShapeDtypeStruct((bsz * seq, n_heads * HEAD_DIM), BF16)]
                  + [jax.ShapeDtypeStruct(w.shape, BF16) for w, _ in riders],
        scratch_shapes=[pltpu.VMEM(tile, F32)] * 6,
        compiler_params=pltpu.CompilerParams(
            dimension_semantics=("parallel", "parallel"), vmem_limit_bytes=VMEM_LIMIT),
        name="dilated_attention",
    )(proj, proj, proj, onw, *[w for w, _ in riders])


def _outproj_kernel(oa_ref, ob_ref, wa_ref, wb_ref, x_ref, fnw_ref, x1_ref, h2_ref):
    acc = (jnp.dot(oa_ref[...], wa_ref[...], preferred_element_type=F32)
           + jnp.dot(ob_ref[...], wb_ref[...], preferred_element_type=F32))
    x1 = x_ref[...] + acc
    x1_ref[...] = x1
    h2 = x1 * lax.rsqrt(jnp.mean(x1 * x1, axis=-1, keepdims=True) + EPS) * fnw_ref[...]
    h2_ref[...] = h2.astype(h2_ref.dtype)


def _outproj(o_a, o_b, w_out, x2, fnw):
    tokens, d_model = x2.shape
    wa_rows = o_a.shape[1]
    tm = 512
    row = lambda i: (i, 0)
    return pl.pallas_call(
        _outproj_kernel,
        grid=(tokens // tm,),
        in_specs=[pl.BlockSpec((tm, wa_rows), row),
                  pl.BlockSpec((tm, o_b.shape[1]), row),
                  pl.BlockSpec((wa_rows, d_model), lambda i: (0, 0)),
                  pl.BlockSpec((o_b.shape[1], d_model), lambda i: (1, 0)),
                  pl.BlockSpec((tm, d_model), row),
                  pl.BlockSpec((1, d_model), lambda i: (0, 0))],
        out_specs=[pl.BlockSpec((tm, d_model), row), pl.BlockSpec((tm, d_model), row)],
        out_shape=[jax.ShapeDtypeStruct((tokens, d_model), F32),
                   jax.ShapeDtypeStruct((tokens, d_model), BF16)],
        compiler_params=pltpu.CompilerParams(
            dimension_semantics=("parallel",), vmem_limit_bytes=VMEM_LIMIT),
        name="outproj",
    )(o_a, o_b, w_out, w_out, x2, fnw)


def _ffn_kernel(h_ref, wg_ref, wu_ref, wd_ref, x1_ref, o_ref):
    @pl.when(pl.program_id(1) == 0)
    def _():
        o_ref[...] = x1_ref[...]

    h = h_ref[...]
    g = jnp.dot(h, wg_ref[...], preferred_element_type=F32)
    u = jnp.dot(h, wu_ref[...], preferred_element_type=F32)
    a = (_silu(g) * u).astype(BF16)
    o_ref[...] += jnp.dot(a, wd_ref[...], preferred_element_type=F32)


def _ffn(h2, w_gate_up, w_down, x1):
    tokens, d_model = x1.shape
    d_ff = w_down.shape[0]
    tm, tf = 1024, 512
    nf = d_ff // tf
    row = lambda i, f: (i, 0)
    return pl.pallas_call(
        _ffn_kernel,
        grid=(tokens // tm, nf),
        in_specs=[pl.BlockSpec((tm, d_model), row),
                  pl.BlockSpec((d_model, tf), lambda i, f: (0, f)),
                  pl.BlockSpec((d_model, tf), lambda i, f: (0, nf + f)),
                  pl.BlockSpec((tf, d_model), lambda i, f: (f, 0)),
                  pl.BlockSpec((tm, d_model), row)],
        out_specs=pl.BlockSpec((tm, d_model), row),
        out_shape=jax.ShapeDtypeStruct((tokens, d_model), F32),
        compiler_params=pltpu.CompilerParams(
            dimension_semantics=("parallel", "arbitrary"), vmem_limit_bytes=VMEM_LIMIT),
        name="swiglu",
    )(h2, w_gate_up, w_gate_up, w_down, x1)


def _pad_lanes(v, offset):
    out = jnp.zeros((1, LANES), F32)
    return lax.dynamic_update_slice(out, v.astype(F32)[None, :], (0, offset))


def kernel(x, positions, attn_norm_w, w_in, conv_w, a_log, dt_bias, delta_out_norm_w, q_norm_w,
           k_norm_w, attn_out_norm_w, w_out, ffn_norm_w, w_gate_up, w_down):
    bsz, seq, d_model = x.shape
    depth = w_in.shape[0]
    n_heads = a_log.shape[1]
    width = n_heads * HEAD_DIM
    gate_lo, gate_hi = 4 * width, 4 * width + 2 * n_heads
    assert 2 * n_heads <= LANES and w_in.shape[2] == gate_hi + 3 * width

    cos_t, sin_t = _rope_tables(positions)
    x2 = x.reshape(bsz * seq, d_model)
    for l in range(depth):
        w_all = w_in[l].astype(BF16)
        proj, gates, gates_t, w_gate_up_bf = _inproj(
            x2, attn_norm_w[l][None, :], w_all, w_all[:, gate_hi:],
            _pad_lanes(a_log[l], n_heads), _pad_lanes(dt_bias[l], n_heads),
            q_norm_w[l][None, :], k_norm_w[l][None, :], cos_t, sin_t, n_heads,
            rider=(w_gate_up[l], 1))
        o_a = _delta(proj, gates, gates_t, conv_w[l], delta_out_norm_w[l][None, :], bsz, seq, n_heads)
        o_b, w_down_bf, w_out_bf = _attention(
            proj, attn_out_norm_w[l][None, :], bsz, seq, n_heads, 4 * n_heads,
            riders=((w_down[l], 0), (w_out[l], 0)))
        x1, h2 = _outproj(o_a, o_b, w_out_bf, x2, ffn_norm_w[l][None, :])
        x2 = _ffn(h2, w_gate_up_bf, w_down_bf, x1)
    return x2.reshape(bsz, seq, d_model)
```

```python
import functools

import jax
import jax.numpy as jnp
from jax import lax
from jax.experimental import pallas as pl
from jax.experimental.pallas import tpu as pltpu

HEAD_DIM = 128
CONV_WIDTH = 4
DILATED_PATTERNS = ((128, 1), (512, 4), (2048, 16))
ROPE_THETA = 10000.0
EPS = 1e-6

LANES = 128
SUBLANES = 8
DELTA_CHUNK = 128
NEG_BIG = -1e30
ATTN_UNROLL = 8

F32 = jnp.float32
BF16 = jnp.bfloat16

VMEM_LIMIT = 60 * 1024 * 1024


def _mm(a, b):
    return jnp.dot(a.astype(BF16), b.astype(BF16), preferred_element_type=F32)


def _mm_nt(a, b):
    return lax.dot_general(a.astype(BF16), b.astype(BF16), (((1,), (1,)), ((), ())),
                           preferred_element_type=F32)


def _split3(x):
    hi = x.astype(BF16)
    rest = x - hi.astype(F32)
    mid = rest.astype(BF16)
    lo = (rest - mid.astype(F32)).astype(BF16)
    return jnp.concatenate([hi, mid, lo], axis=1)


def _silu(x):
    return x * jax.nn.sigmoid(x)


def _inproj_kernel(x_ref, nw_ref, wd_ref, wa_ref, wg_ref, alog_ref, dtb_ref, qnw_ref,
                   knw_ref, pos_ref, invf_ref, ride_ref,
                   out_ref, gates_ref, gates_t_ref, cast_ref, hn_ref, cos_ref, sin_ref, *, n_heads,
                   n_delta_tiles):
    j = pl.program_id(1)

    cast_ref[...] = ride_ref[...].astype(cast_ref.dtype)

    @pl.when(j == 0)
    def _():
        x = x_ref[...]
        y = x * lax.rsqrt(jnp.mean(x * x, axis=-1, keepdims=True) + EPS) * nw_ref[...]
        hn = y.astype(BF16)
        hn_ref[...] = hn
        gl = jnp.dot(hn, wg_ref[...], preferred_element_type=F32)
        lane = lax.broadcasted_iota(jnp.int32, gl.shape, 1)
        beta = jax.nn.sigmoid(gl)
        t = gl + dtb_ref[...]
        softplus = jnp.maximum(t, 0.0) + jnp.log1p(jnp.exp(-jnp.abs(t)))
        g = -jnp.exp(alog_ref[...]) * softplus
        c = DELTA_CHUNK
        tri = (lax.broadcasted_iota(jnp.int32, (c, c), 0)
               >= lax.broadcasted_iota(jnp.int32, (c, c), 1)).astype(BF16)
        g3 = _split3(g)
        parts = jnp.concatenate(
            [jnp.dot(tri, g3[ch * c:(ch + 1) * c], preferred_element_type=F32)
             for ch in range(g.shape[0] // c)], axis=0)
        gc = parts[:, :LANES] + parts[:, LANES:2 * LANES] + parts[:, 2 * LANES:]
        gates = jnp.where(lane < n_heads, beta, gc)
        gates_ref[...] = gates
        gates_t_ref[...] = gates.T[:2 * n_heads, :]

    @pl.when(j < n_delta_tiles)
    def _():
        out_ref[...] = jnp.dot(hn_ref[...], wd_ref[...],
                               preferred_element_type=F32).astype(out_ref.dtype)
        part = cos_ref.shape[0] // n_delta_tiles
        rows = pl.ds(pl.multiple_of(j * part, part), part)
        ang = pos_ref[rows, :].astype(F32) * invf_ref[...]
        lane = lax.broadcasted_iota(jnp.int32, ang.shape, 1)
        sin = jnp.sin(ang)
        cos_ref[rows, :] = jnp.cos(ang)
        sin_ref[rows, :] = jnp.where(lane < HEAD_DIM // 2, -sin, sin)

    @pl.when(j == n_delta_tiles + 2)
    def _():
        out_ref[...] = jnp.dot(hn_ref[...], wa_ref[...],
                               preferred_element_type=F32).astype(out_ref.dtype)

    @pl.when(jnp.logical_or(j == n_delta_tiles, j == n_delta_tiles + 1))
    def _():
        acc = jnp.dot(hn_ref[...], wa_ref[...], preferred_element_type=F32)
        nw = jnp.where(j == n_delta_tiles, qnw_ref[...], knw_ref[...])
        out_scale = jnp.where(j == n_delta_tiles, HEAD_DIM ** -0.5, 1.0)
        half = HEAD_DIM // 2
        nw_rot = jnp.concatenate([nw[:, half:], nw[:, :half]], axis=1)
        cos_w = cos_ref[...] * (nw * out_scale)
        sin_w = sin_ref[...] * (nw_rot * out_scale)
        pi = lax.broadcasted_iota(jnp.int32, (HEAD_DIM, HEAD_DIM), 0)
        pj = lax.broadcasted_iota(jnp.int32, (HEAD_DIM, HEAD_DIM), 1)
        ones_m = jnp.ones((HEAD_DIM, HEAD_DIM), BF16)
        rot_m = (pi == ((pj + half) % HEAD_DIM)).astype(BF16)
        for h in range(n_heads):
            xh = acc[:, h * HEAD_DIM:(h + 1) * HEAD_DIM]
            x_rot = jnp.dot(xh.astype(BF16), rot_m, preferred_element_type=F32)
            mean_sq = jnp.dot((xh * xh).astype(BF16), ones_m,
                              preferred_element_type=F32) * (1.0 / HEAD_DIM)
            roped = lax.rsqrt(mean_sq + EPS) * (xh * cos_w + x_rot * sin_w)
            out_ref[:, h * HEAD_DIM:(h + 1) * HEAD_DIM] = roped.astype(out_ref.dtype)


def _ride_spec(shape, axis, grid):
    align = LANES if axis == 1 else 2 * SUBLANES
    n_steps = grid[0] * grid[1]
    dim = shape[axis]
    size = next(b for b in range(align, dim + 1, align) if dim % b == 0 and dim // b <= n_steps)
    n_blocks = dim // size
    block = tuple(size if a == axis else d for a, d in enumerate(shape))

    def index(i, j):
        t = jnp.minimum(i * grid[1] + j, n_blocks - 1)
        return tuple(t if a == axis else 0 for a in range(len(shape)))

    return pl.BlockSpec(block, index)


def _inproj(x2, nw, w_all, w_attn, alog, dtb, qnw, knw, positions, n_heads, rider):
    tokens, d_model = x2.shape
    width = n_heads * HEAD_DIM
    n_delta_tiles = 4
    n_tiles = n_delta_tiles + w_attn.shape[1] // width
    tm = 1024
    grid = (tokens // tm, n_tiles)
    ride_specs = [_ride_spec(rider[0].shape, rider[1], grid)]
    half = HEAD_DIM // 2
    inv_freq = ROPE_THETA ** (-jnp.arange(half, dtype=F32) / half)
    invf = jnp.concatenate([inv_freq, inv_freq])[None, :]
    kern = functools.partial(_inproj_kernel, n_heads=n_heads, n_delta_tiles=n_delta_tiles)
    row = lambda i, j: (i, 0)
    const = lambda i, j: (0, 0)
    once = dict(pipeline_mode=pl.Buffered(1))
    return pl.pallas_call(
        kern,
        grid=grid,
        in_specs=[pl.BlockSpec((tm, d_model), row),
                  pl.BlockSpec((1, d_model), const),
                  pl.BlockSpec((d_model, width), lambda i, j: (0, jnp.minimum(j, n_delta_tiles - 1))),
                  pl.BlockSpec((d_model, width), lambda i, j: (0, jnp.maximum(j - n_delta_tiles, 0))),
                  pl.BlockSpec((d_model, LANES), lambda i, j: (0, n_delta_tiles * width // LANES), **once),
                  pl.BlockSpec((1, LANES), const),
                  pl.BlockSpec((1, LANES), const),
                  pl.BlockSpec((1, HEAD_DIM), const),
                  pl.BlockSpec((1, HEAD_DIM), const),
                  pl.BlockSpec((tm, 1), row),
                  pl.BlockSpec((1, HEAD_DIM), const)] + ride_specs,
        out_specs=[pl.BlockSpec((tm, width), lambda i, j: (i, j)),
                   pl.BlockSpec((tm, LANES), row),
                   pl.BlockSpec((2 * n_heads, tm), lambda i, j: (0, i))] + ride_specs,
        out_shape=[jax.ShapeDtypeStruct((tokens, n_tiles * width), BF16),
                   jax.ShapeDtypeStruct((tokens, LANES), F32),
                   jax.ShapeDtypeStruct((2 * n_heads, tokens), F32)]
                  + [jax.ShapeDtypeStruct(rider[0].shape, BF16)],
        scratch_shapes=[pltpu.VMEM((tm, d_model), BF16),
                        pltpu.VMEM((tm, HEAD_DIM), F32), pltpu.VMEM((tm, HEAD_DIM), F32)],
        compiler_params=pltpu.CompilerParams(
            dimension_semantics=("arbitrary", "arbitrary"), vmem_limit_bytes=VMEM_LIMIT),
        name="inproj",
    )(x2, nw, w_all, w_attn, w_all, alog, dtb, qnw, knw, positions.reshape(tokens, 1), invf,
      rider[0])


def _delta_kernel(q_ref, k_ref, v_ref, z_ref, gates_ref, gates_t_ref, cwq_ref, cwk_ref, cwv_ref,
                  onw_ref, o_ref, state_ref, carry_ref, xpad_ref, *, n_heads, heads_per_step):
    hg = pl.program_id(1)
    s_idx = pl.program_id(2)
    ts = q_ref.shape[0]
    c = DELTA_CHUNK
    n_chunks = ts // c
    hd = HEAD_DIM

    @pl.when(s_idx == 0)
    def _():
        state_ref[...] = jnp.zeros_like(state_ref)
        carry_ref[...] = jnp.zeros_like(carry_ref)

    def conv_silu(x_ref, cw_ref, slot):
        w = cw_ref[...]
        halves = []
        for hh in range(heads_per_step):
            cols = slice(hh * hd, (hh + 1) * hd)
            xpad_ref[hh, pl.ds(0, SUBLANES, stride=2), :] = carry_ref[slot, :, cols]
            xpad_ref[hh, pl.ds(2 * SUBLANES, ts, stride=2), :] = x_ref[:, cols].astype(F32)
            y = jnp.zeros((ts, hd), F32)
            for j in range(CONV_WIDTH):
                off = 2 * (SUBLANES - (CONV_WIDTH - 1) + j)
                y = y + xpad_ref[hh, pl.ds(off, ts, stride=2), :] * w[j:j + 1, cols]
            halves.append(_silu(y))
            carry_ref[slot, :, cols] = xpad_ref[hh, pl.ds(2 * ts, SUBLANES, stride=2), :]
        return halves

    def l2n(t):
        return t * lax.rsqrt(jnp.sum(t * t, axis=-1, keepdims=True) + EPS)

    q_conv = conv_silu(q_ref, cwq_ref, 0)
    k_conv = conv_silu(k_ref, cwk_ref, 1)
    v_conv = conv_silu(v_ref, cwv_ref, 2)
    gt = gates_ref[...]
    lane = lax.broadcasted_iota(jnp.int32, gt.shape, 1)
    head_iota = lax.broadcasted_iota(jnp.int32, (n_heads, c), 0)

    ri = lax.broadcasted_iota(jnp.int32, (c, c), 0)
    ci = lax.broadcasted_iota(jnp.int32, (c, c), 1)
    incl = ri >= ci
    strict = ri > ci
    eye = (ri == ci).astype(F32)
    onw = onw_ref[...]

    qs, ks, vs, betas, gcbs, gcrs = [], [], [], [], [], []
    for hh in range(heads_per_step):
        head = hg * heads_per_step + hh
        q_h = l2n(q_conv[hh]) * (hd ** -0.5)
        k_h = l2n(k_conv[hh])
        v_h = v_conv[hh]
        beta_h = jnp.broadcast_to(
            jnp.sum(jnp.where(lane == head, gt, 0.0), axis=-1, keepdims=True), (ts, c))
        gc_h = jnp.broadcast_to(
            jnp.sum(jnp.where(lane == head + n_heads, gt, 0.0), axis=-1, keepdims=True), (ts, c))
        for ch in range(n_chunks):
            r = slice(ch * c, (ch + 1) * c)
            qs.append(q_h[r])
            ks.append(k_h[r])
            vs.append(v_h[r])
            betas.append(beta_h[r])
            gcbs.append(gc_h[r])
            gc_rows = gates_t_ref[n_heads:2 * n_heads, r]
            gcrs.append(jnp.sum(jnp.where(head_iota == head, gc_rows, 0.0),
                                axis=0, keepdims=True))
    units = range(heads_per_step * n_chunks)

    kks = [_mm_nt(ks[n], ks[n]) for n in units]
    qks = [_mm_nt(qs[n], ks[n]) for n in units]
    decays = [jnp.where(incl, jnp.exp(jnp.where(incl, gcbs[n] - gcrs[n], 0.0)), 0.0) for n in units]
    lowers = [jnp.where(strict, betas[n] * kks[n] * decays[n], 0.0) for n in units]
    same2 = (ri // 2) == (ci // 2)
    invs = [eye - jnp.where(same2, lowers[n], 0.0) for n in units]
    lowers_bf = [lowers[n].astype(BF16) for n in units]
    b = 2
    while b < c:
        off = jnp.logical_and((ri // (2 * b)) == (ci // (2 * b)), (ri // b) != (ci // b))
        off_bf = off.astype(F32).astype(BF16)
        tmp = [_mm(lowers_bf[n] * off_bf, invs[n]) for n in units]
        invs = [invs[n] - _mm(invs[n], tmp[n]) for n in units]
        b *= 2
    egcs = [jnp.exp(gcbs[n]) for n in units]
    us = [_mm(invs[n], vs[n] * betas[n]) for n in units]
    ws = [_mm(invs[n], ks[n] * (betas[n] * egcs[n])) for n in units]
    qks = [qks[n] * decays[n] for n in units]
    g_lasts = [gcbs[n][c - 1:c, :] for n in units]
    kd_ts = [(ks[n] * jnp.exp(g_lasts[n] - gcbs[n])).T for n in units]
    s_mix = [_mm(kd_ts[n], ws[n]) for n in units]
    s_add = [_mm(kd_ts[n], us[n]) for n in units]
    q_eff = [qs[n] * egcs[n] - _mm(qks[n], ws[n]) for n in units]
    o_loc = [_mm(qks[n], us[n]) for n in units]

    states = [state_ref[hh] for hh in range(heads_per_step)]
    outs = {}
    for ch in range(n_chunks):
        for hh in range(heads_per_step):
            n = hh * n_chunks + ch
            outs[n] = _mm(q_eff[n], states[hh]) + o_loc[n]
            states[hh] = (states[hh] * jnp.exp(g_lasts[n]) - _mm(s_mix[n], states[hh])) + s_add[n]
    for hh in range(heads_per_step):
        state_ref[hh] = states[hh]

    for hh in range(heads_per_step):
        cols = slice(hh * hd, (hh + 1) * hd)
        for ch in range(n_chunks):
            r = slice(ch * c, (ch + 1) * c)
            o_c = outs[hh * n_chunks + ch]
            o_n = o_c * lax.rsqrt(jnp.mean(o_c * o_c, axis=-1, keepdims=True) + EPS) * onw
            o_ref[r, cols] = (o_n * _silu(z_ref[r, cols].astype(F32))).astype(o_ref.dtype)


def _delta(proj, gates, gates_t, conv_w, onw, bsz, seq, n_heads):
    ts = 1024
    hb = 4
    s_tiles = seq // ts
    groups = n_heads // hb
    kern = functools.partial(_delta_kernel, n_heads=n_heads, heads_per_step=hb)

    def col(seg):
        return lambda b, g, s: (b * s_tiles + s, seg * groups + g)

    def cw(seg):
        return lambda b, g, s: (0, seg * groups + g)

    tile = (ts, hb * HEAD_DIM)
    return pl.pallas_call(
        kern,
        grid=(bsz, groups, s_tiles),
        in_specs=[pl.BlockSpec(tile, col(0)), pl.BlockSpec(tile, col(1)),
                  pl.BlockSpec(tile, col(2)), pl.BlockSpec(tile, col(3)),
                  pl.BlockSpec((ts, LANES), lambda b, g, s: (b * s_tiles + s, 0)),
                  pl.BlockSpec((2 * n_heads, ts), lambda b, g, s: (0, b * s_tiles + s)),
                  pl.BlockSpec((CONV_WIDTH, hb * HEAD_DIM), cw(0)),
                  pl.BlockSpec((CONV_WIDTH, hb * HEAD_DIM), cw(1)),
                  pl.BlockSpec((CONV_WIDTH, hb * HEAD_DIM), cw(2)),
                  pl.BlockSpec((1, HEAD_DIM), lambda b, g, s: (0, 0))],
        out_specs=pl.BlockSpec(tile, lambda b, g, s: (b * s_tiles + s, g)),
        out_shape=jax.ShapeDtypeStruct((bsz * seq, n_heads * HEAD_DIM), BF16),
        scratch_shapes=[pltpu.VMEM((hb, HEAD_DIM, HEAD_DIM), F32),
                        pltpu.VMEM((3, SUBLANES, hb * HEAD_DIM), F32),
                        pltpu.VMEM((hb, 2 * (ts + SUBLANES), HEAD_DIM), F32)],
        compiler_params=pltpu.CompilerParams(
            dimension_semantics=("parallel", "parallel", "arbitrary"),
            vmem_limit_bytes=VMEM_LIMIT),
        name="delta_rule",
    )(proj, proj, proj, proj, gates, gates_t, conv_w, conv_w, conv_w, onw)


def _attn_kernel(q_ref, k_ref, v_ref, onw_ref, ride0_ref, ride1_ref, o_ref, cast0_ref, cast1_ref,
                 qf_ref, kf_ref, vf_ref, acc_ref, m_ref, l_ref):
    seq = q_ref.shape[0]
    hd = HEAD_DIM
    onw = onw_ref[...]
    order = sorted(DILATED_PATTERNS, key=lambda p: -p[1])
    cast0_ref[...] = ride0_ref[...].astype(cast0_ref.dtype)
    cast1_ref[...] = ride1_ref[...].astype(cast1_ref.dtype)
    qf_ref[...] = q_ref[...].astype(F32)
    kf_ref[...] = k_ref[...].astype(F32)
    vf_ref[...] = v_ref[...].astype(F32)

    for pi, (window, dil) in enumerate(order):
        first, last = pi == 0, pi == len(order) - 1
        blk = window // dil
        nb = seq // (blk * dil)
        run_len = min(nb, ATTN_UNROLL)
        runs_per_iter = ATTN_UNROLL // run_len
        runs_per_residue = nb // run_len
        qi = lax.broadcasted_iota(jnp.int32, (blk, blk), 0)
        ki = lax.broadcasted_iota(jnp.int32, (blk, blk), 1)
        cur_mask = ki <= qi
        prev_mask = ki >= qi
        ones_cols = jnp.ones((blk, hd), BF16)

        def body(it, carry, first=first, last=last, dil=dil, blk=blk, run_len=run_len,
                 runs_per_iter=runs_per_iter, runs_per_residue=runs_per_residue,
                 cur_mask=cur_mask, prev_mask=prev_mask, ones_cols=ones_cols):
            def rows(start):
                if dil == 1:
                    return pl.ds(pl.multiple_of(start, blk), blk)
                return pl.ds(start, blk, stride=dil)

            def tile(f32_ref, bf16_ref, start):
                if dil == 1:
                    return bf16_ref[rows(start), :]
                return f32_ref[rows(start), :].astype(BF16)

            blocks = []
            for rn in range(runs_per_iter):
                run = it * runs_per_iter + rn
                if runs_per_residue == 1:
                    r, j0 = run, 0
                    prev_k = prev_v = prev_ok = None
                else:
                    r = run // runs_per_residue
                    j0 = (run % runs_per_residue) * run_len
                    pstart = jnp.maximum(j0 - 1, 0) * (blk * dil) + r
                    prev_k, prev_v = tile(kf_ref, k_ref, pstart), tile(vf_ref, v_ref, pstart)
                    prev_ok = j0 > 0
                for jj in range(run_len):
                    start = (j0 + jj) * (blk * dil) + r
                    cur = rows(start)
                    k_t, v_t = tile(kf_ref, k_ref, start), tile(vf_ref, v_ref, start)
                    old = None if first else (m_ref[cur, :], l_ref[cur, :], acc_ref[cur, :])
                    blocks.append((cur, tile(qf_ref, q_ref, start), k_t, v_t, prev_k, prev_v,
                                   prev_ok, old))
                    prev_k, prev_v, prev_ok = k_t, v_t, None

            scores = []
            for cur, q_t, k_t, v_t, prev_k, prev_v, prev_ok, old in blocks:
                if prev_k is None:
                    scores.append((jnp.where(cur_mask, _mm_nt(q_t, k_t), NEG_BIG), None))
                else:
                    s2 = _mm_nt(q_t, jnp.concatenate([k_t, prev_k], axis=0))
                    pm = prev_mask if prev_ok is None else jnp.logical_and(prev_mask, prev_ok)
                    scores.append((jnp.where(cur_mask, s2[:, :blk], NEG_BIG),
                                   jnp.where(pm, s2[:, blk:], NEG_BIG)))

            probs = []
            for (cur, q_t, k_t, v_t, prev_k, prev_v, prev_ok, old), (s_cur, s_prev) in zip(blocks, scores):
                s_max = s_cur if s_prev is None else jnp.maximum(s_cur, s_prev)
                m_new = jnp.max(s_max, axis=-1, keepdims=True)
                if old is not None:
                    m_new = jnp.maximum(old[0], m_new)
                p_cur = jnp.exp(s_cur - m_new).astype(BF16)
                p_prev = None if s_prev is None else jnp.exp(s_prev - m_new).astype(BF16)
                probs.append((m_new, p_cur, p_prev))

            results = []
            for (cur, q_t, k_t, v_t, prev_k, prev_v, prev_ok, old), (m_new, p_cur, p_prev) in zip(blocks, probs):
                v_ext = jnp.concatenate([v_t, ones_cols], axis=1)
                if p_prev is None:
                    pv = jnp.dot(p_cur, v_ext, preferred_element_type=F32)
                else:
                    pv = jnp.dot(jnp.concatenate([p_cur, p_prev], axis=1),
                                 jnp.concatenate(
                                     [v_ext, jnp.concatenate([prev_v, ones_cols], axis=1)], axis=0),
                                 preferred_element_type=F32)
                acc_new, l_new = pv[:, :hd], pv[:, hd:]
                if old is not None:
                    alpha = jnp.exp(old[0] - m_new)
                    l_new = alpha * old[1] + l_new
                    acc_new = alpha * old[2] + acc_new
                results.append((cur, m_new, l_new, acc_new))

            for cur, m_new, l_new, acc_new in results:
                if last:
                    o = acc_new / l_new
                    o = o * lax.rsqrt(jnp.mean(o * o, axis=-1, keepdims=True) + EPS) * onw
                    o_ref[cur, :] = o.astype(o_ref.dtype)
                else:
                    m_ref[cur, :] = jnp.broadcast_to(m_new, (blk, hd))
                    l_ref[cur, :] = l_new
                    acc_ref[cur, :] = acc_new
            return carry

        lax.fori_loop(0, dil * nb // ATTN_UNROLL, body, 0)


def _attention(proj, onw, bsz, seq, n_heads, col0, riders):
    tile = (seq, HEAD_DIM)
    grid = (bsz, n_heads)
    ride_specs = [_ride_spec(w.shape, axis, grid) for w, axis in riders]

    def col(seg):
        return lambda b, h: (b, col0 + seg * n_heads + h)

    return pl.pallas_call(
        _attn_kernel,
        grid=grid,
        in_specs=[pl.BlockSpec(tile, col(0)), pl.BlockSpec(tile, col(1)),
                  pl.BlockSpec(tile, col(2)),
                  pl.BlockSpec((1, HEAD_DIM), lambda b, h: (0, 0))] + ride_specs,
        out_specs=[pl.BlockSpec(tile, lambda b, h: (b, h))] + ride_specs,
        out_shape=[jax.ShapeDtypeStruct((bsz * seq, n_heads * HEAD_DIM), BF16)]
                  + [jax.ShapeDtypeStruct(w.shape, BF16) for w, _ in riders],
        scratch_shapes=[pltpu.VMEM(tile, F32)] * 6,
        compiler_params=pltpu.CompilerParams(
            dimension_semantics=("parallel", "parallel"), vmem_limit_bytes=VMEM_LIMIT),
        name="dilated_attention",
    )(proj, proj, proj, onw, *[w for w, _ in riders])


def _outproj_kernel(oa_ref, ob_ref, wa_ref, wb_ref, x_ref, fnw_ref, x1_ref, h2_ref):
    acc = (jnp.dot(oa_ref[...], wa_ref[...], preferred_element_type=F32)
           + jnp.dot(ob_ref[...], wb_ref[...], preferred_element_type=F32))
    x1 = x_ref[...] + acc
    x1_ref[...] = x1
    h2 = x1 * lax.rsqrt(jnp.mean(x1 * x1, axis=-1, keepdims=True) + EPS) * fnw_ref[...]
    h2_ref[...] = h2.astype(h2_ref.dtype)


def _outproj(o_a, o_b, w_out, x2, fnw):
    tokens, d_model = x2.shape
    wa_rows = o_a.shape[1]
    tm = 512
    row = lambda i: (i, 0)
    return pl.pallas_call(
        _outproj_kernel,
        grid=(tokens // tm,),
        in_specs=[pl.BlockSpec((tm, wa_rows), row),
                  pl.BlockSpec((tm, o_b.shape[1]), row),
                  pl.BlockSpec((wa_rows, d_model), lambda i: (0, 0)),
                  pl.BlockSpec((o_b.shape[1], d_model), lambda i: (1, 0)),
                  pl.BlockSpec((tm, d_model), row),
                  pl.BlockSpec((1, d_model), lambda i: (0, 0))],
        out_specs=[pl.BlockSpec((tm, d_model), row), pl.BlockSpec((tm, d_model), row)],
        out_shape=[jax.ShapeDtypeStruct((tokens, d_model), F32),
                   jax.ShapeDtypeStruct((tokens, d_model), BF16)],
        compiler_params=pltpu.CompilerParams(
            dimension_semantics=("parallel",), vmem_limit_bytes=VMEM_LIMIT),
        name="outproj",
    )(o_a, o_b, w_out, w_out, x2, fnw)


def _ffn_kernel(h_ref, wg_ref, wu_ref, wd_ref, x1_ref, o_ref):
    @pl.when(pl.program_id(1) == 0)
    def _():
        o_ref[...] = x1_ref[...]

    h = h_ref[...]
    g = jnp.dot(h, wg_ref[...], preferred_element_type=F32)
    u = jnp.dot(h, wu_ref[...], preferred_element_type=F32)
    a = (_silu(g) * u).astype(BF16)
    o_ref[...] += jnp.dot(a, wd_ref[...], preferred_element_type=F32)


def _ffn(h2, w_gate_up, w_down, x1):
    tokens, d_model = x1.shape
    d_ff = w_down.shape[0]
    tm, tf = 1024, 512
    nf = d_ff // tf
    row = lambda i, f: (i, 0)
    return pl.pallas_call(
        _ffn_kernel,
        grid=(tokens // tm, nf),
        in_specs=[pl.BlockSpec((tm, d_model), row),
                  pl.BlockSpec((d_model, tf), lambda i, f: (0, f)),
                  pl.BlockSpec((d_model, tf), lambda i, f: (0, nf + f)),
                  pl.BlockSpec((tf, d_model), lambda i, f: (f, 0)),
                  pl.BlockSpec((tm, d_model), row)],
        out_specs=pl.BlockSpec((tm, d_model), row),
        out_shape=jax.ShapeDtypeStruct((tokens, d_model), F32),
        compiler_params=pltpu.CompilerParams(
            dimension_semantics=("parallel", "arbitrary"), vmem_limit_bytes=VMEM_LIMIT),
        name="swiglu",
    )(h2, w_gate_up, w_gate_up, w_down, x1)


def _pad_lanes(v, offset):
    out = jnp.zeros((1, LANES), F32)
    return lax.dynamic_update_slice(out, v.astype(F32)[None, :], (0, offset))


def kernel(x, positions, attn_norm_w, w_in, conv_w, a_log, dt_bias, delta_out_norm_w, q_norm_w,
           k_norm_w, attn_out_norm_w, w_out, ffn_norm_w, w_gate_up, w_down):
    bsz, seq, d_model = x.shape
    depth = w_in.shape[0]
    n_heads = a_log.shape[1]
    width = n_heads * HEAD_DIM
    gate_lo, gate_hi = 4 * width, 4 * width + 2 * n_heads
    assert 2 * n_heads <= LANES and w_in.shape[2] == gate_hi + 3 * width

    x2 = x.reshape(bsz * seq, d_model)
    for l in range(depth):
        w_all = w_in[l].astype(BF16)
        proj, gates, gates_t, w_gate_up_bf = _inproj(
            x2, attn_norm_w[l][None, :], w_all, w_all[:, gate_hi:],
            _pad_lanes(a_log[l], n_heads), _pad_lanes(dt_bias[l], n_heads),
            q_norm_w[l][None, :], k_norm_w[l][None, :], positions, n_heads,
            rider=(w_gate_up[l], 1))
        o_a = _delta(proj, gates, gates_t, conv_w[l], delta_out_norm_w[l][None, :], bsz, seq, n_heads)
        o_b, w_down_bf, w_out_bf = _attention(
            proj, attn_out_norm_w[l][None, :], bsz, seq, n_heads, 4 * n_heads,
            riders=((w_down[l], 0), (w_out[l], 0)))
        x1, h2 = _outproj(o_a, o_b, w_out_bf, x2, ffn_norm_w[l][None, :])
        x2 = _ffn(h2, w_gate_up_bf, w_down_bf, x1)
    return x2.reshape(bsz, seq, d_model)
```

```python
import functools

import jax
import jax.numpy as jnp
from jax import lax
from jax.experimental import pallas as pl
from jax.experimental.pallas import tpu as pltpu

HEAD_DIM = 128
CONV_WIDTH = 4
DILATED_PATTERNS = ((128, 1), (512, 4), (2048, 16))
ROPE_THETA = 10000.0
EPS = 1e-6

LANES = 128
SUBLANES = 8
DELTA_CHUNK = 128
NEG_BIG = -1e30
ATTN_UNROLL = 8

F32 = jnp.float32
BF16 = jnp.bfloat16

VMEM_LIMIT = 62 * 1024 * 1024


def _mm(a, b):
    return jnp.dot(a.astype(BF16), b.astype(BF16), preferred_element_type=F32)


def _mm_nt(a, b):
    return lax.dot_general(a.astype(BF16), b.astype(BF16), (((1,), (1,)), ((), ())),
                           preferred_element_type=F32)


def _split3(x):
    hi = x.astype(BF16)
    rest = x - hi.astype(F32)
    mid = rest.astype(BF16)
    lo = (rest - mid.astype(F32)).astype(BF16)
    return jnp.concatenate([hi, mid, lo], axis=1)


def _silu(x):
    return x * jax.nn.sigmoid(x)


def _inproj_kernel(x_ref, nw_ref, wd_ref, wa_ref, wg_ref, alog_ref, dtb_ref, qnw_ref,
                   knw_ref, pos_ref, invf_ref, ride_ref,
                   out_ref, attn_ref, gates_ref, gates_t_ref, cast_ref, hn_ref, cos_ref, sin_ref, *,
                   n_heads, n_delta_tiles):
    j = pl.program_id(1)

    cast_ref[...] = ride_ref[...].astype(cast_ref.dtype)

    @pl.when(j == 0)
    def _():
        x = x_ref[...]
        y = x * lax.rsqrt(jnp.mean(x * x, axis=-1, keepdims=True) + EPS) * nw_ref[...]
        hn = y.astype(BF16)
        hn_ref[...] = hn
        gl = jnp.dot(hn, wg_ref[...], preferred_element_type=F32)
        lane = lax.broadcasted_iota(jnp.int32, gl.shape, 1)
        beta = jax.nn.sigmoid(gl)
        t = gl + dtb_ref[...]
        softplus = jnp.maximum(t, 0.0) + jnp.log1p(jnp.exp(-jnp.abs(t)))
        g = -jnp.exp(alog_ref[...]) * softplus
        c = DELTA_CHUNK
        tri = (lax.broadcasted_iota(jnp.int32, (c, c), 0)
               >= lax.broadcasted_iota(jnp.int32, (c, c), 1)).astype(BF16)
        g3 = _split3(g)
        parts = jnp.concatenate(
            [jnp.dot(tri, g3[ch * c:(ch + 1) * c], preferred_element_type=F32)
             for ch in range(g.shape[0] // c)], axis=0)
        gc = parts[:, :LANES] + parts[:, LANES:2 * LANES] + parts[:, 2 * LANES:]
        gates = jnp.where(lane < n_heads, beta, gc)
        gates_ref[...] = gates
        gates_t_ref[...] = gates.T[:2 * n_heads, :]

    @pl.when(j < n_delta_tiles)
    def _():
        out_ref[...] = jnp.dot(hn_ref[...], wd_ref[...],
                               preferred_element_type=F32).astype(out_ref.dtype)
        part = cos_ref.shape[0] // n_delta_tiles
        rows = pl.ds(pl.multiple_of(j * part, part), part)
        ang = pos_ref[rows, :].astype(F32) * invf_ref[...]
        lane = lax.broadcasted_iota(jnp.int32, ang.shape, 1)
        sin = jnp.sin(ang)
        cos_ref[rows, :] = jnp.cos(ang)
        sin_ref[rows, :] = jnp.where(lane < HEAD_DIM // 2, -sin, sin)

    @pl.when(j == n_delta_tiles + 2)
    def _():
        acc = jnp.dot(hn_ref[...], wa_ref[...], preferred_element_type=F32)
        for h in range(n_heads):
            attn_ref[h] = acc[:, h * HEAD_DIM:(h + 1) * HEAD_DIM].astype(attn_ref.dtype)

    @pl.when(jnp.logical_or(j == n_delta_tiles, j == n_delta_tiles + 1))
    def _():
        acc = jnp.dot(hn_ref[...], wa_ref[...], preferred_element_type=F32)
        nw = jnp.where(j == n_delta_tiles, qnw_ref[...], knw_ref[...])
        out_scale = jnp.where(j == n_delta_tiles, HEAD_DIM ** -0.5, 1.0)
        half = HEAD_DIM // 2
        nw_rot = jnp.concatenate([nw[:, half:], nw[:, :half]], axis=1)
        cos_w = cos_ref[...] * (nw * out_scale)
        sin_w = sin_ref[...] * (nw_rot * out_scale)
        pi = lax.broadcasted_iota(jnp.int32, (HEAD_DIM, HEAD_DIM), 0)
        pj = lax.broadcasted_iota(jnp.int32, (HEAD_DIM, HEAD_DIM), 1)
        ones_m = jnp.ones((HEAD_DIM, HEAD_DIM), BF16)
        rot_m = (pi == ((pj + half) % HEAD_DIM)).astype(BF16)
        for h in range(n_heads):
            xh = acc[:, h * HEAD_DIM:(h + 1) * HEAD_DIM]
            x_rot = jnp.dot(xh.astype(BF16), rot_m, preferred_element_type=F32)
            mean_sq = jnp.dot((xh * xh).astype(BF16), ones_m,
                              preferred_element_type=F32) * (1.0 / HEAD_DIM)
            roped = lax.rsqrt(mean_sq + EPS) * (xh * cos_w + x_rot * sin_w)
            attn_ref[h] = roped.astype(attn_ref.dtype)


def _ride_spec(shape, axis, grid):
    align = LANES if axis == 1 else 2 * SUBLANES
    n_steps = grid[0] * grid[1]
    dim = shape[axis]
    size = next(b for b in range(align, dim + 1, align) if dim % b == 0 and dim // b <= n_steps)
    n_blocks = dim // size
    block = tuple(size if a == axis else d for a, d in enumerate(shape))

    def index(i, j):
        t = jnp.minimum(i * grid[1] + j, n_blocks - 1)
        return tuple(t if a == axis else 0 for a in range(len(shape)))

    return pl.BlockSpec(block, index)


def _inproj(x2, nw, w_all, w_attn, alog, dtb, qnw, knw, positions, n_heads, rider):
    tokens, d_model = x2.shape
    width = n_heads * HEAD_DIM
    n_delta_tiles = 4
    n_tiles = n_delta_tiles + w_attn.shape[1] // width
    tm = 1024
    grid = (tokens // tm, n_tiles)
    ride_specs = [_ride_spec(rider[0].shape, rider[1], grid)]
    half = HEAD_DIM // 2
    inv_freq = ROPE_THETA ** (-jnp.arange(half, dtype=F32) / half)
    invf = jnp.concatenate([inv_freq, inv_freq])[None, :]
    kern = functools.partial(_inproj_kernel, n_heads=n_heads, n_delta_tiles=n_delta_tiles)
    row = lambda i, j: (i, 0)
    const = lambda i, j: (0, 0)
    once = dict(pipeline_mode=pl.Buffered(1))
    return pl.pallas_call(
        kern,
        grid=grid,
        in_specs=[pl.BlockSpec((tm, d_model), row),
                  pl.BlockSpec((1, d_model), const),
                  pl.BlockSpec((d_model, width), lambda i, j: (0, jnp.minimum(j, n_delta_tiles - 1))),
                  pl.BlockSpec((d_model, width), lambda i, j: (0, jnp.maximum(j - n_delta_tiles, 0))),
                  pl.BlockSpec((d_model, LANES), lambda i, j: (0, n_delta_tiles * width // LANES), **once),
                  pl.BlockSpec((1, LANES), const),
                  pl.BlockSpec((1, LANES), const),
                  pl.BlockSpec((1, HEAD_DIM), const),
                  pl.BlockSpec((1, HEAD_DIM), const),
                  pl.BlockSpec((tm, 1), row),
                  pl.BlockSpec((1, HEAD_DIM), const)] + ride_specs,
        out_specs=[pl.BlockSpec((tm, width), lambda i, j: (i, jnp.minimum(j, n_delta_tiles - 1))),
                   pl.BlockSpec((n_heads, tm, HEAD_DIM),
                                lambda i, j: (jnp.maximum(j - n_delta_tiles, 0), i, 0)),
                   pl.BlockSpec((tm, LANES), row),
                   pl.BlockSpec((2 * n_heads, tm), lambda i, j: (0, i))] + ride_specs,
        out_shape=[jax.ShapeDtypeStruct((tokens, n_delta_tiles * width), BF16),
                   jax.ShapeDtypeStruct(((n_tiles - n_delta_tiles) * n_heads, tokens, HEAD_DIM), BF16),
                   jax.ShapeDtypeStruct((tokens, LANES), F32),
                   jax.ShapeDtypeStruct((2 * n_heads, tokens), F32)]
                  + [jax.ShapeDtypeStruct(rider[0].shape, BF16)],
        scratch_shapes=[pltpu.VMEM((tm, d_model), BF16),
                        pltpu.VMEM((tm, HEAD_DIM), F32), pltpu.VMEM((tm, HEAD_DIM), F32)],
        compiler_params=pltpu.CompilerParams(
            dimension_semantics=("arbitrary", "arbitrary"), vmem_limit_bytes=VMEM_LIMIT),
        name="inproj",
    )(x2, nw, w_all, w_attn, w_all, alog, dtb, qnw, knw, positions.reshape(tokens, 1), invf,
      rider[0])


def _delta_kernel(q_ref, k_ref, v_ref, z_ref, gates_ref, gates_t_ref, cwq_ref, cwk_ref, cwv_ref,
                  onw_ref, o_ref, state_ref, carry_ref, xpad_ref, *, n_heads, heads_per_step):
    hg = pl.program_id(1)
    s_idx = pl.program_id(2)
    ts = q_ref.shape[0]
    c = DELTA_CHUNK
    n_chunks = ts // c
    hd = HEAD_DIM

    @pl.when(s_idx == 0)
    def _():
        state_ref[...] = jnp.zeros_like(state_ref)
        carry_ref[...] = jnp.zeros_like(carry_ref)

    def conv_silu(x_ref, cw_ref, slot):
        w = cw_ref[...]
        halves = []
        for hh in range(heads_per_step):
            cols = slice(hh * hd, (hh + 1) * hd)
            xpad_ref[hh, pl.ds(0, SUBLANES, stride=2), :] = carry_ref[slot, :, cols]
            xpad_ref[hh, pl.ds(2 * SUBLANES, ts, stride=2), :] = x_ref[:, cols].astype(F32)
            y = jnp.zeros((ts, hd), F32)
            for j in range(CONV_WIDTH):
                off = 2 * (SUBLANES - (CONV_WIDTH - 1) + j)
                y = y + xpad_ref[hh, pl.ds(off, ts, stride=2), :] * w[j:j + 1, cols]
            halves.append(_silu(y))
            carry_ref[slot, :, cols] = xpad_ref[hh, pl.ds(2 * ts, SUBLANES, stride=2), :]
        return halves

    def l2n(t):
        return t * lax.rsqrt(jnp.sum(t * t, axis=-1, keepdims=True) + EPS)

    q_conv = conv_silu(q_ref, cwq_ref, 0)
    k_conv = conv_silu(k_ref, cwk_ref, 1)
    v_conv = conv_silu(v_ref, cwv_ref, 2)
    gt = gates_ref[...]
    lane = lax.broadcasted_iota(jnp.int32, gt.shape, 1)
    head_iota = lax.broadcasted_iota(jnp.int32, (n_heads, c), 0)

    ri = lax.broadcasted_iota(jnp.int32, (c, c), 0)
    ci = lax.broadcasted_iota(jnp.int32, (c, c), 1)
    incl = ri >= ci
    strict = ri > ci
    eye = (ri == ci).astype(F32)
    onw = onw_ref[...]

    qs, ks, vs, betas, gcbs, gcrs = [], [], [], [], [], []
    for hh in range(heads_per_step):
        head = hg * heads_per_step + hh
        q_h = l2n(q_conv[hh]) * (hd ** -0.5)
        k_h = l2n(k_conv[hh])
        v_h = v_conv[hh]
        beta_h = jnp.broadcast_to(
            jnp.sum(jnp.where(lane == head, gt, 0.0), axis=-1, keepdims=True), (ts, c))
        gc_h = jnp.broadcast_to(
            jnp.sum(jnp.where(lane == head + n_heads, gt, 0.0), axis=-1, keepdims=True), (ts, c))
        for ch in range(n_chunks):
            r = slice(ch * c, (ch + 1) * c)
            qs.append(q_h[r])
            ks.append(k_h[r])
            vs.append(v_h[r])
            betas.append(beta_h[r])
            gcbs.append(gc_h[r])
            gc_rows = gates_t_ref[n_heads:2 * n_heads, r]
            gcrs.append(jnp.sum(jnp.where(head_iota == head, gc_rows, 0.0),
                                axis=0, keepdims=True))
    units = range(heads_per_step * n_chunks)

    kks = [_mm_nt(ks[n], ks[n]) for n in units]
    qks = [_mm_nt(qs[n], ks[n]) for n in units]
    decays = [jnp.where(incl, jnp.exp(jnp.where(incl, gcbs[n] - gcrs[n], 0.0)), 0.0) for n in units]
    lowers = [jnp.where(strict, betas[n] * kks[n] * decays[n], 0.0) for n in units]
    same2 = (ri // 2) == (ci // 2)
    invs = [eye - jnp.where(same2, lowers[n], 0.0) for n in units]
    lowers_bf = [lowers[n].astype(BF16) for n in units]
    b = 2
    while b < c:
        off = jnp.logical_and((ri // (2 * b)) == (ci // (2 * b)), (ri // b) != (ci // b))
        off_bf = off.astype(F32).astype(BF16)
        tmp = [_mm(lowers_bf[n] * off_bf, invs[n]) for n in units]
        invs = [invs[n] - _mm(invs[n], tmp[n]) for n in units]
        b *= 2
    egcs = [jnp.exp(gcbs[n]) for n in units]
    us = [_mm(invs[n], vs[n] * betas[n]) for n in units]
    ws = [_mm(invs[n], ks[n] * (betas[n] * egcs[n])) for n in units]
    qks = [qks[n] * decays[n] for n in units]
    g_lasts = [gcbs[n][c - 1:c, :] for n in units]
    kd_ts = [(ks[n] * jnp.exp(g_lasts[n] - gcbs[n])).T for n in units]
    s_mix = [_mm(kd_ts[n], ws[n]) for n in units]
    s_add = [_mm(kd_ts[n], us[n]) for n in units]
    q_eff = [qs[n] * egcs[n] - _mm(qks[n], ws[n]) for n in units]
    o_loc = [_mm(qks[n], us[n]) for n in units]

    states = [state_ref[hh] for hh in range(heads_per_step)]
    outs = {}
    for ch in range(n_chunks):
        for hh in range(heads_per_step):
            n = hh * n_chunks + ch
            outs[n] = _mm(q_eff[n], states[hh]) + o_loc[n]
            states[hh] = (states[hh] * jnp.exp(g_lasts[n]) - _mm(s_mix[n], states[hh])) + s_add[n]
    for hh in range(heads_per_step):
        state_ref[hh] = states[hh]

    for hh in range(heads_per_step):
        cols = slice(hh * hd, (hh + 1) * hd)
        for ch in range(n_chunks):
            r = slice(ch * c, (ch + 1) * c)
            o_c = outs[hh * n_chunks + ch]
            o_n = o_c * lax.rsqrt(jnp.mean(o_c * o_c, axis=-1, keepdims=True) + EPS) * onw
            o_ref[r, cols] = (o_n * _silu(z_ref[r, cols].astype(F32))).astype(o_ref.dtype)


def _delta(proj, gates, gates_t, conv_w, onw, bsz, seq, n_heads):
    ts = 1024
    hb = 4
    s_tiles = seq // ts
    groups = n_heads // hb
    kern = functools.partial(_delta_kernel, n_heads=n_heads, heads_per_step=hb)

    def col(seg):
        return lambda b, g, s: (b * s_tiles + s, seg * groups + g)

    def cw(seg):
        return lambda b, g, s: (0, seg * groups + g)

    tile = (ts, hb * HEAD_DIM)
    return pl.pallas_call(
        kern,
        grid=(bsz, groups, s_tiles),
        in_specs=[pl.BlockSpec(tile, col(0)), pl.BlockSpec(tile, col(1)),
                  pl.BlockSpec(tile, col(2)), pl.BlockSpec(tile, col(3)),
                  pl.BlockSpec((ts, LANES), lambda b, g, s: (b * s_tiles + s, 0)),
                  pl.BlockSpec((2 * n_heads, ts), lambda b, g, s: (0, b * s_tiles + s)),
                  pl.BlockSpec((CONV_WIDTH, hb * HEAD_DIM), cw(0)),
                  pl.BlockSpec((CONV_WIDTH, hb * HEAD_DIM), cw(1)),
                  pl.BlockSpec((CONV_WIDTH, hb * HEAD_DIM), cw(2)),
                  pl.BlockSpec((1, HEAD_DIM), lambda b, g, s: (0, 0))],
        out_specs=pl.BlockSpec(tile, lambda b, g, s: (b * s_tiles + s, g)),
        out_shape=jax.ShapeDtypeStruct((bsz * seq, n_heads * HEAD_DIM), BF16),
        scratch_shapes=[pltpu.VMEM((hb, HEAD_DIM, HEAD_DIM), F32),
                        pltpu.VMEM((3, SUBLANES, hb * HEAD_DIM), F32),
                        pltpu.VMEM((hb, 2 * (ts + SUBLANES), HEAD_DIM), F32)],
        compiler_params=pltpu.CompilerParams(
            dimension_semantics=("parallel", "parallel", "arbitrary"),
            vmem_limit_bytes=VMEM_LIMIT),
        name="delta_rule",
    )(proj, proj, proj, proj, gates, gates_t, conv_w, conv_w, conv_w, onw)


def _attn_kernel(q_ref, k_ref, v_ref, onw_ref, ride0_ref, ride1_ref, o_ref, cast0_ref, cast1_ref,
                 qf_ref, kf_ref, vf_ref, acc_ref, m_ref, l_ref):
    seq = q_ref.shape[0]
    hd = HEAD_DIM
    onw = onw_ref[...]
    order = sorted(DILATED_PATTERNS, key=lambda p: -p[1])
    cast0_ref[...] = ride0_ref[...].astype(cast0_ref.dtype)
    cast1_ref[...] = ride1_ref[...].astype(cast1_ref.dtype)
    qf_ref[...] = q_ref[...].astype(F32)
    kf_ref[...] = k_ref[...].astype(F32)
    vf_ref[...] = v_ref[...].astype(F32)

    for pi, (window, dil) in enumerate(order):
        first, last = pi == 0, pi == len(order) - 1
        blk = window // dil
        nb = seq // (blk * dil)
        run_len = min(nb, ATTN_UNROLL)
        runs_per_iter = ATTN_UNROLL // run_len
        runs_per_residue = nb // run_len
        qi = lax.broadcasted_iota(jnp.int32, (blk, blk), 0)
        ki = lax.broadcasted_iota(jnp.int32, (blk, blk), 1)
        cur_mask = ki <= qi
        prev_mask = ki >= qi
        ones_cols = jnp.ones((blk, hd), BF16)

        def body(it, carry, first=first, last=last, dil=dil, blk=blk, run_len=run_len,
                 runs_per_iter=runs_per_iter, runs_per_residue=runs_per_residue,
                 cur_mask=cur_mask, prev_mask=prev_mask, ones_cols=ones_cols):
            def rows(start):
                if dil == 1:
                    return pl.ds(pl.multiple_of(start, blk), blk)
                return pl.ds(start, blk, stride=dil)

            def tile(f32_ref, bf16_ref, start):
                if dil == 1:
                    return bf16_ref[rows(start), :]
                return f32_ref[rows(start), :].astype(BF16)

            blocks = []
            for rn in range(runs_per_iter):
                run = it * runs_per_iter + rn
                if runs_per_residue == 1:
                    r, j0 = run, 0
                    prev_k = prev_v = prev_ok = None
                else:
                    r = run // runs_per_residue
                    j0 = (run % runs_per_residue) * run_len
                    pstart = jnp.maximum(j0 - 1, 0) * (blk * dil) + r
                    prev_k, prev_v = tile(kf_ref, k_ref, pstart), tile(vf_ref, v_ref, pstart)
                    prev_ok = j0 > 0
                for jj in range(run_len):
                    start = (j0 + jj) * (blk * dil) + r
                    cur = rows(start)
                    k_t, v_t = tile(kf_ref, k_ref, start), tile(vf_ref, v_ref, start)
                    old = None if first else (m_ref[cur, :], l_ref[cur, :], acc_ref[cur, :])
                    blocks.append((cur, tile(qf_ref, q_ref, start), k_t, v_t, prev_k, prev_v,
                                   prev_ok, old))
                    prev_k, prev_v, prev_ok = k_t, v_t, None

            scores = []
            for cur, q_t, k_t, v_t, prev_k, prev_v, prev_ok, old in blocks:
                if prev_k is None:
                    scores.append((jnp.where(cur_mask, _mm_nt(q_t, k_t), NEG_BIG), None))
                else:
                    s2 = _mm_nt(q_t, jnp.concatenate([k_t, prev_k], axis=0))
                    pm = prev_mask if prev_ok is None else jnp.logical_and(prev_mask, prev_ok)
                    scores.append((jnp.where(cur_mask, s2[:, :blk], NEG_BIG),
                                   jnp.where(pm, s2[:, blk:], NEG_BIG)))

            probs = []
            for (cur, q_t, k_t, v_t, prev_k, prev_v, prev_ok, old), (s_cur, s_prev) in zip(blocks, scores):
                s_max = s_cur if s_prev is None else jnp.maximum(s_cur, s_prev)
                m_new = jnp.max(s_max, axis=-1, keepdims=True)
                if old is not None:
                    m_new = jnp.maximum(old[0], m_new)
                p_cur = jnp.exp(s_cur - m_new).astype(BF16)
                p_prev = None if s_prev is None else jnp.exp(s_prev - m_new).astype(BF16)
                probs.append((m_new, p_cur, p_prev))

            results = []
            for (cur, q_t, k_t, v_t, prev_k, prev_v, prev_ok, old), (m_new, p_cur, p_prev) in zip(blocks, probs):
                v_ext = jnp.concatenate([v_t, ones_cols], axis=1)
                if p_prev is None:
                    pv = jnp.dot(p_cur, v_ext, preferred_element_type=F32)
                else:
                    pv = jnp.dot(jnp.concatenate([p_cur, p_prev], axis=1),
                                 jnp.concatenate(
                                     [v_ext, jnp.concatenate([prev_v, ones_cols], axis=1)], axis=0),
                                 preferred_element_type=F32)
                acc_new, l_new = pv[:, :hd], pv[:, hd:]
                if old is not None:
                    alpha = jnp.exp(old[0] - m_new)
                    l_new = alpha * old[1] + l_new
                    acc_new = alpha * old[2] + acc_new
                results.append((cur, m_new, l_new, acc_new))

            for cur, m_new, l_new, acc_new in results:
                if last:
                    o = acc_new / l_new
                    o = o * lax.rsqrt(jnp.mean(o * o, axis=-1, keepdims=True) + EPS) * onw
                    o_ref[cur, :] = o.astype(o_ref.dtype)
                else:
                    m_ref[cur, :] = jnp.broadcast_to(m_new, (blk, hd))
                    l_ref[cur, :] = l_new
                    acc_ref[cur, :] = acc_new
            return carry

        lax.fori_loop(0, dil * nb // ATTN_UNROLL, body, 0)


def _attention(qkv, onw, bsz, seq, n_heads, riders):
    tile = (seq, HEAD_DIM)
    grid = (bsz, n_heads)
    ride_specs = [_ride_spec(w.shape, axis, grid) for w, axis in riders]

    def head(seg):
        return pl.BlockSpec((None, seq, HEAD_DIM), lambda b, h: (seg * n_heads + h, b, 0))

    return pl.pallas_call(
        _attn_kernel,
        grid=grid,
        in_specs=[head(0), head(1), head(2),
                  pl.BlockSpec((1, HEAD_DIM), lambda b, h: (0, 0))] + ride_specs,
        out_specs=[pl.BlockSpec(tile, lambda b, h: (b, h))] + ride_specs,
        out_shape=[jax.ShapeDtypeStruct((bsz * seq, n_heads * HEAD_DIM), BF16)]
                  + [jax.ShapeDtypeStruct(w.shape, BF16) for w, _ in riders],
        scratch_shapes=[pltpu.VMEM(tile, F32)] * 6,
        compiler_params=pltpu.CompilerParams(
            dimension_semantics=("parallel", "parallel"), vmem_limit_bytes=VMEM_LIMIT),
        name="dilated_attention",
    )(qkv, qkv, qkv, onw, *[w for w, _ in riders])


def _outproj_kernel(oa_ref, ob_ref, wa_ref, wb_ref, x_ref, fnw_ref, x1_ref, h2_ref):
    acc = (jnp.dot(oa_ref[...], wa_ref[...], preferred_element_type=F32)
           + jnp.dot(ob_ref[...], wb_ref[...], preferred_element_type=F32))
    x1 = x_ref[...] + acc
    x1_ref[...] = x1
    h2 = x1 * lax.rsqrt(jnp.mean(x1 * x1, axis=-1, keepdims=True) + EPS) * fnw_ref[...]
    h2_ref[...] = h2.astype(h2_ref.dtype)


def _outproj(o_a, o_b, w_out, x2, fnw):
    tokens, d_model = x2.shape
    wa_rows = o_a.shape[1]
    tm = 512
    row = lambda i: (i, 0)
    return pl.pallas_call(
        _outproj_kernel,
        grid=(tokens // tm,),
        in_specs=[pl.BlockSpec((tm, wa_rows), row),
                  pl.BlockSpec((tm, o_b.shape[1]), row),
                  pl.BlockSpec((wa_rows, d_model), lambda i: (0, 0)),
                  pl.BlockSpec((o_b.shape[1], d_model), lambda i: (1, 0)),
                  pl.BlockSpec((tm, d_model), row),
                  pl.BlockSpec((1, d_model), lambda i: (0, 0))],
        out_specs=[pl.BlockSpec((tm, d_model), row), pl.BlockSpec((tm, d_model), row)],
        out_shape=[jax.ShapeDtypeStruct((tokens, d_model), F32),
                   jax.ShapeDtypeStruct((tokens, d_model), BF16)],
        compiler_params=pltpu.CompilerParams(
            dimension_semantics=("parallel",), vmem_limit_bytes=VMEM_LIMIT),
        name="outproj",
    )(o_a, o_b, w_out, w_out, x2, fnw)


def _ffn_kernel(h_ref, wg_ref, wu_ref, wd_ref, x1_ref, o_ref):
    @pl.when(pl.program_id(1) == 0)
    def _():
        o_ref[...] = x1_ref[...]

    h = h_ref[...]
    g = jnp.dot(h, wg_ref[...], preferred_element_type=F32)
    u = jnp.dot(h, wu_ref[...], preferred_element_type=F32)
    a = (_silu(g) * u).astype(BF16)
    o_ref[...] += jnp.dot(a, wd_ref[...], preferred_element_type=F32)


def _ffn(h2, w_gate_up, w_down, x1):
    tokens, d_model = x1.shape
    d_ff = w_down.shape[0]
    tm, tf = 1024, 512
    nf = d_ff // tf
    row = lambda i, f: (i, 0)
    return pl.pallas_call(
        _ffn_kernel,
        grid=(tokens // tm, nf),
        in_specs=[pl.BlockSpec((tm, d_model), row),
                  pl.BlockSpec((d_model, tf), lambda i, f: (0, f)),
                  pl.BlockSpec((d_model, tf), lambda i, f: (0, nf + f)),
                  pl.BlockSpec((tf, d_model), lambda i, f: (f, 0)),
                  pl.BlockSpec((tm, d_model), row)],
        out_specs=pl.BlockSpec((tm, d_model), row),
        out_shape=jax.ShapeDtypeStruct((tokens, d_model), F32),
        compiler_params=pltpu.CompilerParams(
            dimension_semantics=("parallel", "arbitrary"), vmem_limit_bytes=VMEM_LIMIT),
        name="swiglu",
    )(h2, w_gate_up, w_gate_up, w_down, x1)


def _pad_lanes(v, offset):
    out = jnp.zeros((1, LANES), F32)
    return lax.dynamic_update_slice(out, v.astype(F32)[None, :], (0, offset))


def kernel(x, positions, attn_norm_w, w_in, conv_w, a_log, dt_bias, delta_out_norm_w, q_norm_w,
           k_norm_w, attn_out_norm_w, w_out, ffn_norm_w, w_gate_up, w_down):
    bsz, seq, d_model = x.shape
    depth = w_in.shape[0]
    n_heads = a_log.shape[1]
    width = n_heads * HEAD_DIM
    gate_lo, gate_hi = 4 * width, 4 * width + 2 * n_heads
    assert 2 * n_heads <= LANES and w_in.shape[2] == gate_hi + 3 * width

    x2 = x.reshape(bsz * seq, d_model)
    for l in range(depth):
        w_all = w_in[l].astype(BF16)
        proj, qkv, gates, gates_t, w_gate_up_bf = _inproj(
            x2, attn_norm_w[l][None, :], w_all, w_all[:, gate_hi:],
            _pad_lanes(a_log[l], n_heads), _pad_lanes(dt_bias[l], n_heads),
            q_norm_w[l][None, :], k_norm_w[l][None, :], positions, n_heads,
            rider=(w_gate_up[l], 1))
        o_a = _delta(proj, gates, gates_t, conv_w[l], delta_out_norm_w[l][None, :], bsz, seq, n_heads)
        o_b, w_down_bf, w_out_bf = _attention(
            qkv, attn_out_norm_w[l][None, :], bsz, seq, n_heads,
            riders=((w_down[l], 0), (w_out[l], 0)))
        x1, h2 = _outproj(o_a, o_b, w_out_bf, x2, ffn_norm_w[l][None, :])
        x2 = _ffn(h2, w_gate_up_bf, w_down_bf, x1)
    return x2.reshape(bsz, seq, d_model)
```
